```python
import math
import jax, jax.numpy as jnp
from jax import lax
import numpy as np

D_MODEL = 1024
BATCH = 16
SEQ = 2048
DEPTH = 2

GRID_W = 64
CTX_LEN = 256
N_MIXERS = 2
BLOCK = 128
WINDOW = 128
HEAD_DIM = 64
A_HEADS = D_MODEL // HEAD_DIM
A_KV_HEADS = max(1, A_HEADS // 8)
B_HEADS = D_MODEL // (2 * HEAD_DIM)
D_FF = 4 * D_MODEL
N_MOD = 6
ROPE_BASE = 10000.0
LN_EPS = 1e-5
SUBLN_EPS = 1e-5
NEG_INF = -1e30
DEEPNORM_ALPHA = (2 * DEPTH) ** 0.25
DEEPNORM_BETA = (8 * DEPTH) ** -0.25
N_A_LAYERS = (DEPTH + 1) // 2
N_B_LAYERS = DEPTH // 2

kernel_name = 'hybrid_swa_sink_diffattn_dit_block'


def layer_norm(x, g, b):
    x32 = x.astype(jnp.float32)
    mu = jnp.mean(x32, axis=-1, keepdims=True)
    var = jnp.mean(jnp.square(x32 - mu), axis=-1, keepdims=True)
    return ((x32 - mu) * lax.rsqrt(var + LN_EPS) * g + b).astype(x.dtype)


def axial_rope_tables(L):
    rows = L // GRID_W
    row = jnp.repeat(jnp.arange(rows, dtype=jnp.float32), GRID_W)
    col = jnp.tile(jnp.arange(GRID_W, dtype=jnp.float32), rows)
    n_freq = HEAD_DIM // 4
    inv = ROPE_BASE ** (-jnp.arange(n_freq, dtype=jnp.float32) / n_freq)
    ang = jnp.concatenate([row[:, None] * inv, col[:, None] * inv], axis=-1)
    return jnp.cos(ang), jnp.sin(ang)


def apply_rope(x, cos, sin):
    L = x.shape[1]
    shp = (1, L) + (1,) * (x.ndim - 3) + (HEAD_DIM // 2,)
    cs = cos.reshape(shp).astype(x.dtype)
    sn = sin.reshape(shp).astype(x.dtype)
    x1, x2 = jnp.split(x, 2, axis=-1)
    return jnp.concatenate([x1 * cs - x2 * sn, x1 * sn + x2 * cs], axis=-1)


def softmax_with_sink(s, sink):
    m = jnp.maximum(jnp.max(s, axis=-1, keepdims=True), sink)
    p = jnp.exp(s - m)
    return p / (jnp.sum(p, axis=-1, keepdims=True) + jnp.exp(sink - m))


def window_gqa(hx, hc, wq, wk, wv, wo, sink, cos, sin, need_ctx):
    B, L, _ = hx.shape
    C = hc.shape[1]
    G = A_HEADS // A_KV_HEADS
    nb = L // BLOCK
    scale = HEAD_DIM ** -0.5
    q = apply_rope((hx @ wq).reshape(B, L, A_KV_HEADS, G, HEAD_DIM), cos, sin)
    k = apply_rope((hx @ wk).reshape(B, L, A_KV_HEADS, HEAD_DIM), cos, sin)
    v = (hx @ wv).reshape(B, L, A_KV_HEADS, HEAD_DIM)
    kc = (hc @ wk).reshape(B, C, A_KV_HEADS, HEAD_DIM)
    vc = (hc @ wv).reshape(B, C, A_KV_HEADS, HEAD_DIM)
    sink_f = sink.astype(jnp.float32).reshape(A_KV_HEADS, G, 1, 1)
    pad = ((0, 0), (BLOCK, BLOCK), (0, 0), (0, 0))
    kp = jnp.pad(k, pad)
    vp = jnp.pad(v, pad)
    qb = jnp.moveaxis(q.reshape(B, nb, BLOCK, A_KV_HEADS, G, HEAD_DIM), 1, 0)
    qi = jnp.arange(BLOCK)[:, None]
    kj = jnp.arange(3 * BLOCK)[None, :]
    rel = qi + BLOCK - kj

    def block_fn(args):
        n, qblk = args
        kblk = lax.dynamic_slice_in_dim(kp, n * BLOCK, 3 * BLOCK, axis=1)
        vblk = lax.dynamic_slice_in_dim(vp, n * BLOCK, 3 * BLOCK, axis=1)
        kpos = n * BLOCK - BLOCK + kj
        valid = (jnp.abs(rel) <= WINDOW) & (kpos >= 0) & (kpos < L)
        s_lat = jnp.einsum('bqhgd,bkhd->bhgqk', qblk, kblk).astype(jnp.float32) * scale
        s_lat = jnp.where(valid, s_lat, NEG_INF)
        s_ctx = jnp.einsum('bqhgd,bchd->bhgqc', qblk, kc).astype(jnp.float32) * scale
        p = softmax_with_sink(jnp.concatenate([s_lat, s_ctx], axis=-1), sink_f).astype(v.dtype)
        o = jnp.einsum('bhgqk,bkhd->bqhgd', p[..., :3 * BLOCK], vblk)
        return o + jnp.einsum('bhgqc,bchd->bqhgd', p[..., 3 * BLOCK:], vc)

    ob = lax.map(block_fn, (jnp.arange(nb), qb))
    out_x = jnp.moveaxis(ob, 0, 1).reshape(B, L, A_HEADS * HEAD_DIM) @ wo
    out_c = None
    if need_ctx:
        qc = (hc @ wq).reshape(B, C, A_KV_HEADS, G, HEAD_DIM)
        s = jnp.einsum('bqhgd,bkhd->bhgqk', qc, kc).astype(jnp.float32) * scale
        p = softmax_with_sink(s, sink_f).astype(vc.dtype)
        out_c = jnp.einsum('bhgqk,bkhd->bqhgd', p, vc).reshape(B, C, A_HEADS * HEAD_DIM) @ wo
    return out_x, out_c


def diff_attention(hx, hc, wq, wk, wv, wo, lq1, lk1, lq2, lk2, subln_g, lam_init, cos, sin, need_ctx):
    B, L, _ = hx.shape
    C = hc.shape[1]
    H, d = B_HEADS, HEAD_DIM
    nb = L // BLOCK
    scale = d ** -0.5
    q = apply_rope((hx @ wq).reshape(B, L, H, 2, d), cos, sin)
    k = apply_rope((hx @ wk).reshape(B, L, H, 2, d), cos, sin)
    v = (hx @ wv).reshape(B, L, H, 2 * d)
    kc = (hc @ wk).reshape(B, C, H, 2, d)
    vc = (hc @ wv).reshape(B, C, H, 2 * d)
    lam = (jnp.exp(jnp.sum((lq1 * lk1).astype(jnp.float32)))
           - jnp.exp(jnp.sum((lq2 * lk2).astype(jnp.float32))) + lam_init)
    k_all = jnp.concatenate([k, kc], axis=1)
    v_all = jnp.concatenate([v, vc], axis=1)

    def diff_weights(qblk, keys):
        s = jnp.einsum('bqhtd,bkhtd->bhtqk', qblk, keys).astype(jnp.float32) * scale
        p = jax.nn.softmax(s, axis=-1)
        return p[:, :, 0] - lam * p[:, :, 1]

    def block_fn(qblk):
        a = diff_weights(qblk, k_all).astype(v.dtype)
        return jnp.einsum('bhqk,bkhe->bqhe', a, v_all)

    qb = jnp.moveaxis(q.reshape(B, nb, BLOCK, H, 2, d), 1, 0)
    ob = lax.map(block_fn, qb)
    ox = jnp.moveaxis(ob, 0, 1).reshape(B, L, H, 2 * d)

    def head_out(o):
        n = o.shape[1]
        o32 = o.astype(jnp.float32)
        o32 = o32 * lax.rsqrt(jnp.mean(jnp.square(o32), axis=-1, keepdims=True) + SUBLN_EPS)
        o32 = o32 * subln_g * (1.0 - lam_init)
        return o32.astype(o.dtype).reshape(B, n, H * 2 * d) @ wo

    out_x = head_out(ox)
    out_c = None
    if need_ctx:
        qc = (hc @ wq).reshape(B, C, H, 2, d)
        a = diff_weights(qc, kc).astype(vc.dtype)
        out_c = head_out(jnp.einsum('bhqk,bkhe->bqhe', a, vc))
    return out_x, out_c


def sqrelu_mlp(h, w1, w2):
    return jnp.square(jax.nn.relu(h @ w1)) @ w2


def setup_inputs(seed: int = 0) -> dict:
    key = jax.random.key(seed)
    ks = jax.random.split(key, 32)
    D = D_MODEL
    f32 = jnp.float32

    def nrm(k, shape, fan_in, gain=1.0):
        return jax.random.normal(k, shape, f32) * (gain * fan_in ** -0.5)

    def small(k, shape, s):
        return jax.random.normal(k, shape, f32) * s

    return {
        'x': jax.random.normal(ks[0], (BATCH, SEQ, D), f32),
        'c': jax.random.normal(ks[1], (BATCH, D), f32),
        'ctx': jax.random.normal(ks[2], (BATCH, CTX_LEN, D), f32),
        'c_ctx': jax.random.normal(ks[3], (D,), f32),
        'w_ada': nrm(ks[4], (DEPTH, D, N_MOD * D), D, 0.5),
        'b_ada': small(ks[5], (DEPTH, N_MOD * D), 0.02),
        'ln1_g': 1.0 + small(ks[6], (DEPTH, D), 0.05),
        'ln1_b': small(ks[7], (DEPTH, D), 0.02),
        'ln2_g': 1.0 + small(ks[8], (DEPTH, D), 0.05),
        'ln2_b': small(ks[9], (DEPTH, D), 0.02),
        'a_wq': nrm(ks[10], (N_A_LAYERS, D, A_HEADS * HEAD_DIM), D),
        'a_wk': nrm(ks[11], (N_A_LAYERS, D, A_KV_HEADS * HEAD_DIM), D),
        'a_wv': nrm(ks[12], (N_A_LAYERS, D, A_KV_HEADS * HEAD_DIM), D),
        'a_wo': nrm(ks[13], (N_A_LAYERS, A_HEADS * HEAD_DIM, D), A_HEADS * HEAD_DIM, DEEPNORM_BETA),
        'a_sink': small(ks[14], (N_A_LAYERS, A_HEADS), 1.0),
        'b_wq': nrm(ks[15], (N_B_LAYERS, D, B_HEADS * 2 * HEAD_DIM), D),
        'b_wk': nrm(ks[16], (N_B_LAYERS, D, B_HEADS * 2 * HEAD_DIM), D),
        'b_wv': nrm(ks[17], (N_B_LAYERS, D, B_HEADS * 2 * HEAD_DIM), D),
        'b_wo': nrm(ks[18], (N_B_LAYERS, B_HEADS * 2 * HEAD_DIM, D), B_HEADS * 2 * HEAD_DIM, DEEPNORM_BETA),
        'b_lq1': small(ks[19], (N_B_LAYERS, HEAD_DIM), 0.1),
        'b_lk1': small(ks[20], (N_B_LAYERS, HEAD_DIM), 0.1),
        'b_lq2': small(ks[21], (N_B_LAYERS, HEAD_DIM), 0.1),
        'b_lk2': small(ks[22], (N_B_LAYERS, HEAD_DIM), 0.1),
        'b_subln_g': 1.0 + small(ks[23], (N_B_LAYERS, 2 * HEAD_DIM), 0.05),
        'mlp_w1': nrm(ks[24], (DEPTH, D, D_FF), D),
        'mlp_w2': nrm(ks[25], (DEPTH, D_FF, D), D_FF, DEEPNORM_BETA),
    }


def reference(x, c, ctx, c_ctx, w_ada, b_ada, ln1_g, ln1_b, ln2_g, ln2_b,
              a_wq, a_wk, a_wv, a_wo, a_sink,
              b_wq, b_wk, b_wv, b_wo, b_lq1, b_lk1, b_lq2, b_lk2, b_subln_g,
              mlp_w1, mlp_w2):
    L = x.shape[1]
    cos, sin = axial_rope_tables(L)
    alpha = DEEPNORM_ALPHA
    for i in range(DEPTH):
        need_ctx = i < DEPTH - 1
        mod_x = jax.nn.silu(c) @ w_ada[i] + b_ada[i]
        mod_c = jax.nn.silu(c_ctx) @ w_ada[i] + b_ada[i]
        sh1, sc1, g1, sh2, sc2, g2 = jnp.split(mod_x[:, None, :], N_MOD, axis=-1)
        csh1, csc1, cg1, csh2, csc2, cg2 = jnp.split(mod_c, N_MOD, axis=-1)
        hx = x * (1.0 + sc1) + sh1
        hc = ctx * (1.0 + csc1) + csh1
        j = i // N_MIXERS
        if i % N_MIXERS == 0:
            ax, ac = window_gqa(hx, hc, a_wq[j], a_wk[j], a_wv[j], a_wo[j], a_sink[j], cos, sin, need_ctx)
        else:
            lam_init = 0.8 - 0.6 * math.exp(-0.3 * i)
            ax, ac = diff_attention(hx, hc, b_wq[j], b_wk[j], b_wv[j], b_wo[j],
                                    b_lq1[j], b_lk1[j], b_lq2[j], b_lk2[j], b_subln_g[j],
                                    lam_init, cos, sin, need_ctx)
        x = layer_norm(alpha * x + g1 * ax, ln1_g[i], ln1_b[i])
        fx = sqrelu_mlp(x * (1.0 + sc2) + sh2, mlp_w1[i], mlp_w2[i])
        x = layer_norm(alpha * x + g2 * fx, ln2_g[i], ln2_b[i])
        if need_ctx:
            ctx = layer_norm(alpha * ctx + cg1 * ac, ln1_g[i], ln1_b[i])
            fc = sqrelu_mlp(ctx * (1.0 + csc2) + csh2, mlp_w1[i], mlp_w2[i])
            ctx = layer_norm(alpha * ctx + cg2 * fc, ln2_g[i], ln2_b[i])
    return x
```

```python
import functools
import math

import jax
import jax.numpy as jnp
from jax import lax
from jax.experimental import pallas as pl
from jax.experimental.pallas import tpu as pltpu

D_MODEL = 1024
DEPTH = 2
GRID_W = 64
N_MIXERS = 2
BLOCK = 128
HEAD_DIM = 64
A_HEADS = D_MODEL // HEAD_DIM
A_KV_HEADS = max(1, A_HEADS // 8)
A_GROUP = A_HEADS // A_KV_HEADS
B_HEADS = D_MODEL // (2 * HEAD_DIM)
D_FF = 4 * D_MODEL
N_MOD = 6
ROPE_BASE = 10000.0
LN_EPS = 1e-5
SUBLN_EPS = 1e-5
NEG_INF = -1e30
DEEPNORM_ALPHA = (2 * DEPTH) ** 0.25
QK_SCALE = HEAD_DIM ** -0.5

V7X_LANES = 128
V7X_VMEM_BYTES = 64 * 1024 * 1024
MOD_ROWS = 24

BF16 = jnp.bfloat16
F32 = jnp.float32


def _vmem_limit(estimate_bytes):
    return int(min(max(2 * estimate_bytes, 16 * 1024 * 1024), V7X_VMEM_BYTES - 8 * 1024 * 1024))


def _params(estimate_bytes, n_grid):
    return pltpu.CompilerParams(
        dimension_semantics=("arbitrary",) * n_grid,
        vmem_limit_bytes=_vmem_limit(estimate_bytes),
    )


def _layer_norm(y, g, b):
    mu = jnp.mean(y, axis=-1, keepdims=True)
    yc = y - mu
    var = jnp.mean(yc * yc, axis=-1, keepdims=True)
    return yc * lax.rsqrt(var + LN_EPS) * g + b


def _ada_kernel(c_ref, w_ref, b_ref, o_ref):
    c = c_ref[...]
    h = (c * jax.nn.sigmoid(c)).astype(BF16)
    acc = jnp.dot(h, w_ref[0].astype(BF16), preferred_element_type=F32)
    o_ref[0] = acc + b_ref[0]


def _ada(cond, w_ada, b_ada):
    depth, d, n = w_ada.shape
    tn = 1024
    est = 2 * d * tn * 4 + d * tn * 2 + MOD_ROWS * (d + 2 * tn) * 4
    return pl.pallas_call(
        _ada_kernel,
        grid=(depth, n // tn),
        in_specs=[
            pl.BlockSpec((MOD_ROWS, d), lambda i, j: (0, 0)),
            pl.BlockSpec((1, d, tn), lambda i, j: (i, 0, j)),
            pl.BlockSpec((1, 1, tn), lambda i, j: (i, 0, j)),
        ],
        out_specs=pl.BlockSpec((1, MOD_ROWS, tn), lambda i, j: (i, 0, j)),
        out_shape=jax.ShapeDtypeStruct((depth, MOD_ROWS, n), F32),
        compiler_params=_params(est, 2),
        name="ada_mod",
    )(cond, w_ada, b_ada)


def _rope(y, cos, sin_signed, first_half):
    half = HEAD_DIM // 2
    outs = []
    for g in range(y.shape[1] // V7X_LANES):
        yg = y[:, g * V7X_LANES:(g + 1) * V7X_LANES]
        partner = jnp.where(first_half,
                            pltpu.roll(yg, V7X_LANES - half, 1),
                            pltpu.roll(yg, half, 1))
        outs.append(yg * cos + partner * sin_signed)
    return outs


def _proj_kernel(*refs, n_out, rope, scales):
    x_ref, mod_ref = refs[0], refs[1]
    w_refs = refs[2:2 + n_out]
    pos = 2 + n_out
    if rope:
        cos_ref, sin_ref = refs[pos], refs[pos + 1]
        pos += 2
    o_refs = refs[pos:pos + n_out]
    d = x_ref.shape[-1]
    shift = mod_ref[0, :, 0:d]
    scale = mod_ref[0, :, d:2 * d]
    h = (x_ref[0] * (1.0 + scale) + shift).astype(BF16)
    if rope:
        cos = cos_ref[...]
        sin_signed = sin_ref[...]
        lane = lax.broadcasted_iota(jnp.int32, cos.shape, 1)
        first_half = (lane % HEAD_DIM) < (HEAD_DIM // 2)
    for j in range(n_out):
        y = jnp.dot(h, w_refs[j][...], preferred_element_type=F32)
        roped = rope and scales[j] is not None
        if roped:
            parts = _rope(y, cos, sin_signed, first_half)
            for g, part in enumerate(parts):
                if scales[j] != 1.0:
                    part = part * scales[j]
                o_refs[j][0, :, g * V7X_LANES:(g + 1) * V7X_LANES] = part.astype(BF16)
        else:
            if scales[j] is not None and scales[j] != 1.0:
                y = y * scales[j]
            o_refs[j][0] = y.astype(BF16)


def _project(x, mod, weights, rope_tables, scales, tm):
    nb, rows, d = x.shape
    n_out = len(weights)
    rope = rope_tables is not None
    mod_map = (lambda b, i: (b, 0, 0)) if mod.shape[0] == nb and nb > 1 else (lambda b, i: (0, 0, 0))
    in_specs = [
        pl.BlockSpec((1, tm, d), lambda b, i: (b, i, 0)),
        pl.BlockSpec((1, 1, mod.shape[-1]), mod_map),
    ]
    for w in weights:
        in_specs.append(pl.BlockSpec(w.shape, lambda b, i: (0, 0)))
    args = [x, mod] + list(weights)
    if rope:
        in_specs += [pl.BlockSpec((tm, V7X_LANES), lambda b, i: (i, 0))] * 2
        args += list(rope_tables)
    n_tot = sum(w.shape[1] for w in weights)
    est = 2 * tm * d * 4 + 2 * d * n_tot * 2 + 2 * tm * n_tot * 2 + tm * d * 2 + 2 * tm * n_tot * 4
    return pl.pallas_call(
        functools.partial(_proj_kernel, n_out=n_out, rope=rope, scales=tuple(scales)),
        grid=(nb, rows // tm),
        in_specs=in_specs,
        out_specs=[pl.BlockSpec((1, tm, w.shape[1]), lambda b, i: (b, i, 0)) for w in weights],
        out_shape=[jax.ShapeDtypeStruct((nb, rows, w.shape[1]), BF16) for w in weights],
        compiler_params=_params(est, 2),
        name="mod_proj_rope" if rope else "mod_proj",
    )(*args)


def _gqa_heads(q, keys, vals, valid, sink_ref, kv_head, o_ref):
    tq = q.shape[0]
    heads = [kv_head * A_GROUP + h for h in range(A_GROUP)]
    qs = jnp.concatenate([q[:, hh * HEAD_DIM:(hh + 1) * HEAD_DIM] for hh in heads], axis=0)
    s = lax.dot_general(qs, keys, (((1,), (1,)), ((), ())), preferred_element_type=F32)
    if valid is not None:
        s = jnp.where(jnp.concatenate([valid] * A_GROUP, axis=0), s, NEG_INF)
    sink = jnp.concatenate([jnp.full((tq, 1), sink_ref[hh], F32) for hh in heads], axis=0)
    m = jnp.maximum(jnp.max(s, axis=-1, keepdims=True), sink)
    p = jnp.exp(s - m)
    denom = jnp.sum(p, axis=-1, keepdims=True) + jnp.exp(sink - m)
    o = jnp.dot(p.astype(BF16), vals, preferred_element_type=F32) / denom
    for h, hh in enumerate(heads):
        o_ref[0, :, hh * HEAD_DIM:(hh + 1) * HEAD_DIM] = o[h * tq:(h + 1) * tq].astype(BF16)


def _window_attn_kernel(sink_ref, q_ref, kp_ref, kn_ref, kx_ref, vp_ref, vn_ref, vx_ref,
                        kc_ref, vc_ref, o_ref):
    n = pl.program_id(1)
    nblk = pl.num_programs(1)
    ctx_len = kc_ref.shape[1]
    row = lax.broadcasted_iota(jnp.int32, (BLOCK, BLOCK), 0)
    col = lax.broadcasted_iota(jnp.int32, (BLOCK, BLOCK), 1)
    valid = jnp.concatenate([
        (col >= row) & (n > 0),
        jnp.ones((BLOCK, BLOCK), jnp.bool_),
        (col <= row) & (n < nblk - 1),
        jnp.ones((BLOCK, ctx_len), jnp.bool_),
    ], axis=1)
    q = q_ref[0]
    for g in range(A_KV_HEADS):
        sl = slice(g * HEAD_DIM, (g + 1) * HEAD_DIM)
        keys = jnp.concatenate([kp_ref[0, :, sl], kn_ref[0, :, sl], kx_ref[0, :, sl], kc_ref[0, :, sl]], axis=0)
        vals = jnp.concatenate([vp_ref[0, :, sl], vn_ref[0, :, sl], vx_ref[0, :, sl], vc_ref[0, :, sl]], axis=0)
        _gqa_heads(q, keys, vals, valid, sink_ref, g, o_ref)


def _window_attention(q, k, v, kc, vc, sink):
    b, l, dq = q.shape
    dk = k.shape[-1]
    c = kc.shape[1]
    nblk = l // BLOCK
    prev_map = lambda bi, n: (bi, jnp.maximum(n - 1, 0), 0)
    cur_map = lambda bi, n: (bi, n, 0)
    next_map = lambda bi, n: (bi, jnp.minimum(n + 1, nblk - 1), 0)
    ctx_map = lambda bi, n: (bi, 0, 0)
    kv_spec = lambda m: pl.BlockSpec((1, BLOCK, dk), m)
    est = 4 * BLOCK * dq * 2 + 16 * BLOCK * dk * 2 + 6 * A_GROUP * BLOCK * (3 * BLOCK + c) * 4
    return pl.pallas_call(
        _window_attn_kernel,
        grid=(b, nblk),
        in_specs=[
            pl.BlockSpec(memory_space=pltpu.SMEM),
            pl.BlockSpec((1, BLOCK, dq), cur_map),
            kv_spec(prev_map), kv_spec(cur_map), kv_spec(next_map),
            kv_spec(prev_map), kv_spec(cur_map), kv_spec(next_map),
            pl.BlockSpec((1, c, dk), ctx_map),
            pl.BlockSpec((1, c, dk), ctx_map),
        ],
        out_specs=pl.BlockSpec((1, BLOCK, dq), cur_map),
        out_shape=jax.ShapeDtypeStruct((b, l, dq), BF16),
        compiler_params=_params(est, 2),
        name="window_gqa_attn",
    )(sink, q, k, k, k, v, v, v, kc, vc)


def _ctx_attn_kernel(sink_ref, q_ref, kc_ref, vc_ref, o_ref):
    q = q_ref[0]
    for g in range(A_KV_HEADS):
        sl = slice(g * HEAD_DIM, (g + 1) * HEAD_DIM)
        _gqa_heads(q, kc_ref[0, :, sl], vc_ref[0, :, sl], None, sink_ref, g, o_ref)


def _ctx_attention(qc, kc, vc, sink):
    b, c, dq = qc.shape
    dk = kc.shape[-1]
    est = 4 * c * dq * 2 + 4 * c * dk * 2 + 6 * A_GROUP * c * c * 4
    return pl.pallas_call(
        _ctx_attn_kernel,
        grid=(b,),
        in_specs=[
            pl.BlockSpec(memory_space=pltpu.SMEM),
            pl.BlockSpec((1, c, dq), lambda bi: (bi, 0, 0)),
            pl.BlockSpec((1, c, dk), lambda bi: (bi, 0, 0)),
            pl.BlockSpec((1, c, dk), lambda bi: (bi, 0, 0)),
        ],
        out_specs=pl.BlockSpec((1, c, dq), lambda bi: (bi, 0, 0)),
        out_shape=jax.ShapeDtypeStruct((b, c, dq), BF16),
        compiler_params=_params(est, 1),
        name="ctx_gqa_attn",
    )(sink, qc, kc, vc)


def _diff_attn_kernel(lq1_ref, lk1_ref, lq2_ref, lk2_ref, g_ref, q_ref, k_ref, v_ref, kc_ref, vc_ref,
                      o_ref, *, lam_init):
    lam = (jnp.exp(jnp.sum(lq1_ref[...] * lk1_ref[...], axis=-1, keepdims=True))
           - jnp.exp(jnp.sum(lq2_ref[...] * lk2_ref[...], axis=-1, keepdims=True)) + lam_init)
    q = q_ref[0]
    k = k_ref[0]
    kc = kc_ref[0]
    lane = lax.broadcasted_iota(jnp.int32, q.shape, 1)
    contract_last = (((1,), (1,)), ((), ()))
    probs = []
    for t in range(2):
        in_map = (lane >= t * HEAD_DIM) & (lane < (t + 1) * HEAD_DIM)
        qt = jnp.where(in_map, q, jnp.zeros_like(q))
        s_lat = lax.dot_general(qt, k, contract_last, preferred_element_type=F32)
        s_ctx = lax.dot_general(qt, kc, contract_last, preferred_element_type=F32)
        m = jnp.maximum(jnp.max(s_lat, axis=-1, keepdims=True), jnp.max(s_ctx, axis=-1, keepdims=True))
        p_lat = jnp.exp(s_lat - m)
        p_ctx = jnp.exp(s_ctx - m)
        inv = 1.0 / (jnp.sum(p_lat, axis=-1, keepdims=True) + jnp.sum(p_ctx, axis=-1, keepdims=True))
        probs.append((p_lat, p_ctx, inv))
    w0 = probs[0][2]
    w1 = lam * probs[1][2]
    a_lat = (probs[0][0] * w0 - probs[1][0] * w1).astype(BF16)
    a_ctx = (probs[0][1] * w0 - probs[1][1] * w1).astype(BF16)
    o = (jnp.dot(a_lat, v_ref[0], preferred_element_type=F32)
         + jnp.dot(a_ctx, vc_ref[0], preferred_element_type=F32))
    o = o * lax.rsqrt(jnp.mean(o * o, axis=-1, keepdims=True) + SUBLN_EPS)
    o_ref[0] = (o * g_ref[...] * (1.0 - lam_init)).astype(BF16)


def _diff_attention(q, k, v, kc, vc, lq1, lk1, lq2, lk2, subln_g, lam_init, tq):
    b, l, dm = q.shape
    c = kc.shape[1]
    hd = 2 * HEAD_DIM
    vec = lambda n: pl.BlockSpec((1, n), lambda bi, h, i: (0, 0))
    per_head = lambda rows: pl.BlockSpec((1, rows, hd), lambda bi, h, i: (bi, 0, h))
    est = 4 * tq * hd * 2 + 8 * (l + c) * hd * 2 + 8 * tq * (l + c) * 4
    return pl.pallas_call(
        functools.partial(_diff_attn_kernel, lam_init=lam_init),
        grid=(b, dm // hd, l // tq),
        in_specs=[
            vec(HEAD_DIM), vec(HEAD_DIM), vec(HEAD_DIM), vec(HEAD_DIM), vec(hd),
            pl.BlockSpec((1, tq, hd), lambda bi, h, i: (bi, i, h)),
            per_head(l), per_head(l), per_head(c), per_head(c),
        ],
        out_specs=pl.BlockSpec((1, tq, hd), lambda bi, h, i: (bi, i, h)),
        out_shape=jax.ShapeDtypeStruct((b, l, dm), BF16),
        compiler_params=_params(est, 3),
        name="diff_attn",
    )(lq1, lk1, lq2, lk2, subln_g, q, k, v, kc, vc)


def _post_kernel(a_ref, x_ref, mod_ref, wo_ref, w1_ref, w2_ref, ln_ref, o_ref, *, ff_chunk):
    d = x_ref.shape[-1]
    gate1 = mod_ref[0, :, 2 * d:3 * d]
    shift2 = mod_ref[0, :, 3 * d:4 * d]
    scale2 = mod_ref[0, :, 4 * d:5 * d]
    gate2 = mod_ref[0, :, 5 * d:6 * d]
    ax = jnp.dot(a_ref[0], wo_ref[...], preferred_element_type=F32)
    x1 = _layer_norm(DEEPNORM_ALPHA * x_ref[0] + gate1 * ax, ln_ref[0:1, :], ln_ref[1:2, :])
    h = (x1 * (1.0 + scale2) + shift2).astype(BF16)
    fx = jnp.zeros_like(x1)
    for j in range(w1_ref.shape[1] // ff_chunk):
        u = jnp.dot(h, w1_ref[:, j * ff_chunk:(j + 1) * ff_chunk], preferred_element_type=F32)
        u = jnp.maximum(u, 0.0)
        fx = fx + jnp.dot((u * u).astype(BF16), w2_ref[j * ff_chunk:(j + 1) * ff_chunk, :],
                          preferred_element_type=F32)
    o_ref[0] = _layer_norm(DEEPNORM_ALPHA * x1 + gate2 * fx, ln_ref[2:3, :], ln_ref[3:4, :])


def _post(attn, x, mod, wo, w1, w2, ln, tm):
    nb, rows, d = x.shape
    dff = w1.shape[1]
    ff_chunk = 1024
    mod_map = (lambda b, i: (b, 0, 0)) if mod.shape[0] == nb and nb > 1 else (lambda b, i: (0, 0, 0))
    const = lambda shape: pl.BlockSpec(shape, lambda b, i: (0, 0), pipeline_mode=pl.Buffered(1))
    est = ((d * d + 2 * d * dff) * 2 + 2 * tm * d * (2 + 4 + 4)
           + tm * d * 4 * 4 + tm * ff_chunk * 6)
    return pl.pallas_call(
        functools.partial(_post_kernel, ff_chunk=ff_chunk),
        grid=(nb, rows // tm),
        in_specs=[
            pl.BlockSpec((1, tm, d), lambda b, i: (b, i, 0)),
            pl.BlockSpec((1, tm, d), lambda b, i: (b, i, 0)),
            pl.BlockSpec((1, 1, mod.shape[-1]), mod_map),
            const(wo.shape), const(w1.shape), const(w2.shape),
            pl.BlockSpec(ln.shape, lambda b, i: (0, 0)),
        ],
        out_specs=pl.BlockSpec((1, tm, d), lambda b, i: (b, i, 0)),
        out_shape=jax.ShapeDtypeStruct((nb, rows, d), F32),
        compiler_params=_params(est, 2),
        name="oproj_ln_mlp_ln",
    )(attn, x, mod, wo, w1, w2, ln)


def _rope_tables(seq):
    rows = seq // GRID_W
    row = jnp.repeat(jnp.arange(rows, dtype=F32), GRID_W)
    col = jnp.tile(jnp.arange(GRID_W, dtype=F32), rows)
    n_freq = HEAD_DIM // 4
    inv = ROPE_BASE ** (-jnp.arange(n_freq, dtype=F32) / n_freq)
    ang = jnp.concatenate([row[:, None] * inv, col[:, None] * inv], axis=-1)
    cos, sin = jnp.cos(ang), jnp.sin(ang)
    reps = V7X_LANES // HEAD_DIM
    cos_t = jnp.tile(jnp.concatenate([cos, cos], axis=-1), (1, reps))
    sin_t = jnp.tile(jnp.concatenate([-sin, sin], axis=-1), (1, reps))
    return cos_t, sin_t


def kernel(x, c, ctx, c_ctx, w_ada, b_ada, ln1_g, ln1_b, ln2_g, ln2_b, a_wq, a_wk, a_wv, a_wo, a_sink,
           b_wq, b_wk, b_wv, b_wo, b_lq1, b_lk1, b_lq2, b_lk2, b_subln_g, mlp_w1, mlp_w2):
    batch, seq, d = x.shape
    ctx_len = ctx.shape[1]
    tables = _rope_tables(seq)
    bf = lambda w: w.astype(BF16)

    cond = jnp.concatenate([c, c_ctx[None, :], jnp.zeros((MOD_ROWS - batch - 1, d), F32)], axis=0)
    mod = _ada(cond, w_ada, b_ada[:, None, :])

    for i in range(DEPTH):
        need_ctx = i < DEPTH - 1
        mod_x = mod[i, :batch][:, None, :]
        mod_c = mod[i, batch:batch + 1][:, None, :]
        ln = jnp.stack([ln1_g[i], ln1_b[i], ln2_g[i], ln2_b[i]], axis=0)
        j = i // N_MIXERS
        if i % N_MIXERS == 0:
            wq, wk, wv, wo = bf(a_wq[j]), bf(a_wk[j]), bf(a_wv[j]), bf(a_wo[j])
            q, k, v = _project(x, mod_x, [wq, wk, wv], tables, [QK_SCALE, 1.0, None], tm=512)
            if need_ctx:
                qc, kc, vc = _project(ctx, mod_c, [wq, wk, wv], None, [QK_SCALE, None, None], tm=ctx_len)
            else:
                kc, vc = _project(ctx, mod_c, [wk, wv], None, [None, None], tm=ctx_len)
            attn_x = _window_attention(q, k, v, kc, vc, a_sink[j])
            if need_ctx:
                attn_c = _ctx_attention(qc, kc, vc, a_sink[j])
        else:
            lam_init = 0.8 - 0.6 * math.exp(-0.3 * i)
            wq, wk, wv, wo = bf(b_wq[j]), bf(b_wk[j]), bf(b_wv[j]), bf(b_wo[j])
            q, k, v = _project(x, mod_x, [wq, wk, wv], tables, [QK_SCALE, 1.0, None], tm=512)
            if need_ctx:
                raise NotImplementedError("context queries through differential attention")
            kc, vc = _project(ctx, mod_c, [wk, wv], None, [None, None], tm=ctx_len)
            attn_x = _diff_attention(q, k, v, kc, vc, b_lq1[j][None], b_lk1[j][None], b_lq2[j][None],
                                     b_lk2[j][None], b_subln_g[j][None], lam_init, tq=256)
        w1, w2 = bf(mlp_w1[i]), bf(mlp_w2[i])
        x = _post(attn_x, x, mod_x, wo, w1, w2, ln, tm=512)
        if need_ctx:
            ctx = _post(attn_c, ctx, mod_c, wo, w1, w2, ln, tm=ctx_len)
    return x
```

```python
import functools
import math

import jax
import jax.numpy as jnp
from jax import lax
from jax.experimental import pallas as pl
from jax.experimental.pallas import tpu as pltpu

D_MODEL = 1024
DEPTH = 2
GRID_W = 64
N_MIXERS = 2
BLOCK = 128
HEAD_DIM = 64
A_HEADS = D_MODEL // HEAD_DIM
A_KV_HEADS = max(1, A_HEADS // 8)
A_GROUP = A_HEADS // A_KV_HEADS
B_HEADS = D_MODEL // (2 * HEAD_DIM)
D_FF = 4 * D_MODEL
N_MOD = 6
ROPE_BASE = 10000.0
LN_EPS = 1e-5
SUBLN_EPS = 1e-5
NEG_INF = -1e30
DEEPNORM_ALPHA = (2 * DEPTH) ** 0.25
QK_SCALE = HEAD_DIM ** -0.5
LOG2_E = math.log2(math.e)

V7X_LANES = 128
V7X_VMEM_BYTES = 64 * 1024 * 1024
MOD_ROWS = 24

BF16 = jnp.bfloat16
F32 = jnp.float32


def _vmem_limit(estimate_bytes):
    return int(min(max(2 * estimate_bytes, 16 * 1024 * 1024), V7X_VMEM_BYTES - 8 * 1024 * 1024))


def _params(estimate_bytes, n_grid):
    return pltpu.CompilerParams(
        dimension_semantics=("arbitrary",) * n_grid,
        vmem_limit_bytes=_vmem_limit(estimate_bytes),
    )


def _layer_norm(y, g, b):
    mu = jnp.mean(y, axis=-1, keepdims=True)
    yc = y - mu
    var = jnp.mean(yc * yc, axis=-1, keepdims=True)
    return yc * lax.rsqrt(var + LN_EPS) * g + b


def _ada_kernel(c_ref, w_ref, b_ref, o_ref):
    c = c_ref[...]
    h = (c * jax.nn.sigmoid(c)).astype(BF16)
    acc = jnp.dot(h, w_ref[0].astype(BF16), preferred_element_type=F32)
    o_ref[0] = acc + b_ref[0]


def _ada(cond, w_ada, b_ada):
    depth, d, n = w_ada.shape
    tn = 1024
    est = 2 * d * tn * 4 + d * tn * 2 + MOD_ROWS * (d + 2 * tn) * 4
    return pl.pallas_call(
        _ada_kernel,
        grid=(depth, n // tn),
        in_specs=[
            pl.BlockSpec((MOD_ROWS, d), lambda i, j: (0, 0)),
            pl.BlockSpec((1, d, tn), lambda i, j: (i, 0, j)),
            pl.BlockSpec((1, 1, tn), lambda i, j: (i, 0, j)),
        ],
        out_specs=pl.BlockSpec((1, MOD_ROWS, tn), lambda i, j: (i, 0, j)),
        out_shape=jax.ShapeDtypeStruct((depth, MOD_ROWS, n), F32),
        compiler_params=_params(est, 2),
        name="ada_mod",
    )(cond, w_ada, b_ada)


def _rope(y, cos, sin_signed, first_half):
    half = HEAD_DIM // 2
    outs = []
    for g in range(y.shape[1] // V7X_LANES):
        yg = y[:, g * V7X_LANES:(g + 1) * V7X_LANES]
        partner = jnp.where(first_half,
                            pltpu.roll(yg, V7X_LANES - half, 1),
                            pltpu.roll(yg, half, 1))
        outs.append(yg * cos + partner * sin_signed)
    return outs


def _proj_kernel(*refs, n_out, rope, scales):
    x_ref, mod_ref = refs[0], refs[1]
    w_refs = refs[2:2 + n_out]
    pos = 2 + n_out
    if rope:
        cos_ref, sin_ref = refs[pos], refs[pos + 1]
        pos += 2
    o_refs = refs[pos:pos + n_out]
    d = x_ref.shape[-1]
    shift = mod_ref[0, :, 0:d]
    scale = mod_ref[0, :, d:2 * d]
    h = (x_ref[0] * (1.0 + scale) + shift).astype(BF16)
    if rope:
        cos = cos_ref[...]
        sin_signed = sin_ref[...]
        lane = lax.broadcasted_iota(jnp.int32, cos.shape, 1)
        first_half = (lane % HEAD_DIM) < (HEAD_DIM // 2)
    for j in range(n_out):
        y = jnp.dot(h, w_refs[j][...], preferred_element_type=F32)
        roped = rope and scales[j] is not None
        if roped:
            parts = _rope(y, cos, sin_signed, first_half)
            for g, part in enumerate(parts):
                if scales[j] != 1.0:
                    part = part * scales[j]
                o_refs[j][0, :, g * V7X_LANES:(g + 1) * V7X_LANES] = part.astype(BF16)
        else:
            if scales[j] is not None and scales[j] != 1.0:
                y = y * scales[j]
            o_refs[j][0] = y.astype(BF16)


def _project(x, mod, weights, rope_tables, scales, tm):
    nb, rows, d = x.shape
    n_out = len(weights)
    rope = rope_tables is not None
    mod_map = (lambda b, i: (b, 0, 0)) if mod.shape[0] == nb and nb > 1 else (lambda b, i: (0, 0, 0))
    in_specs = [
        pl.BlockSpec((1, tm, d), lambda b, i: (b, i, 0)),
        pl.BlockSpec((1, 1, mod.shape[-1]), mod_map),
    ]
    for w in weights:
        in_specs.append(pl.BlockSpec(w.shape, lambda b, i: (0, 0)))
    args = [x, mod] + list(weights)
    if rope:
        in_specs += [pl.BlockSpec((tm, V7X_LANES), lambda b, i: (i, 0))] * 2
        args += list(rope_tables)
    n_tot = sum(w.shape[1] for w in weights)
    est = 2 * tm * d * 4 + 2 * d * n_tot * 2 + 2 * tm * n_tot * 2 + tm * d * 2 + 2 * tm * n_tot * 4
    return pl.pallas_call(
        functools.partial(_proj_kernel, n_out=n_out, rope=rope, scales=tuple(scales)),
        grid=(nb, rows // tm),
        in_specs=in_specs,
        out_specs=[pl.BlockSpec((1, tm, w.shape[1]), lambda b, i: (b, i, 0)) for w in weights],
        out_shape=[jax.ShapeDtypeStruct((nb, rows, w.shape[1]), BF16) for w in weights],
        compiler_params=_params(est, 2),
        name="mod_proj_rope" if rope else "mod_proj",
    )(*args)


def _gqa_heads(q, keys, vals, valid, sink_ref, kv_head, o_ref):
    tq = q.shape[0]
    heads = [kv_head * A_GROUP + h for h in range(A_GROUP)]
    qs = jnp.concatenate([q[:, hh * HEAD_DIM:(hh + 1) * HEAD_DIM] for hh in heads], axis=0)
    s = lax.dot_general(qs, keys, (((1,), (1,)), ((), ())), preferred_element_type=F32)
    if valid is not None:
        s = jnp.where(jnp.concatenate([valid] * A_GROUP, axis=0), s, NEG_INF)
    sink = jnp.concatenate([jnp.full((tq, 1), sink_ref[hh], F32) for hh in heads], axis=0)
    m = jnp.maximum(jnp.max(s, axis=-1, keepdims=True), sink)
    p = jnp.exp(s - m)
    denom = jnp.sum(p, axis=-1, keepdims=True) + jnp.exp(sink - m)
    o = jnp.dot(p.astype(BF16), vals, preferred_element_type=F32) / denom
    for h, hh in enumerate(heads):
        o_ref[0, :, hh * HEAD_DIM:(hh + 1) * HEAD_DIM] = o[h * tq:(h + 1) * tq].astype(BF16)


def _window_attn_kernel(sink_ref, q_ref, kp_ref, kn_ref, kx_ref, vp_ref, vn_ref, vx_ref,
                        kc_ref, vc_ref, o_ref):
    n = pl.program_id(1)
    nblk = pl.num_programs(1)
    ctx_len = kc_ref.shape[1]
    row = lax.broadcasted_iota(jnp.int32, (BLOCK, BLOCK), 0)
    col = lax.broadcasted_iota(jnp.int32, (BLOCK, BLOCK), 1)
    valid = jnp.concatenate([
        (col >= row) & (n > 0),
        jnp.ones((BLOCK, BLOCK), jnp.bool_),
        (col <= row) & (n < nblk - 1),
        jnp.ones((BLOCK, ctx_len), jnp.bool_),
    ], axis=1)
    q = q_ref[0]
    for g in range(A_KV_HEADS):
        sl = slice(g * HEAD_DIM, (g + 1) * HEAD_DIM)
        keys = jnp.concatenate([kp_ref[0, :, sl], kn_ref[0, :, sl], kx_ref[0, :, sl], kc_ref[0, :, sl]], axis=0)
        vals = jnp.concatenate([vp_ref[0, :, sl], vn_ref[0, :, sl], vx_ref[0, :, sl], vc_ref[0, :, sl]], axis=0)
        _gqa_heads(q, keys, vals, valid, sink_ref, g, o_ref)


def _window_attention(q, k, v, kc, vc, sink):
    b, l, dq = q.shape
    dk = k.shape[-1]
    c = kc.shape[1]
    nblk = l // BLOCK
    prev_map = lambda bi, n: (bi, jnp.maximum(n - 1, 0), 0)
    cur_map = lambda bi, n: (bi, n, 0)
    next_map = lambda bi, n: (bi, jnp.minimum(n + 1, nblk - 1), 0)
    ctx_map = lambda bi, n: (bi, 0, 0)
    kv_spec = lambda m: pl.BlockSpec((1, BLOCK, dk), m)
    est = 4 * BLOCK * dq * 2 + 16 * BLOCK * dk * 2 + 6 * A_GROUP * BLOCK * (3 * BLOCK + c) * 4
    return pl.pallas_call(
        _window_attn_kernel,
        grid=(b, nblk),
        in_specs=[
            pl.BlockSpec(memory_space=pltpu.SMEM),
            pl.BlockSpec((1, BLOCK, dq), cur_map),
            kv_spec(prev_map), kv_spec(cur_map), kv_spec(next_map),
            kv_spec(prev_map), kv_spec(cur_map), kv_spec(next_map),
            pl.BlockSpec((1, c, dk), ctx_map),
            pl.BlockSpec((1, c, dk), ctx_map),
        ],
        out_specs=pl.BlockSpec((1, BLOCK, dq), cur_map),
        out_shape=jax.ShapeDtypeStruct((b, l, dq), BF16),
        compiler_params=_params(est, 2),
        name="window_gqa_attn",
    )(sink, q, k, k, k, v, v, v, kc, vc)


def _ctx_attn_kernel(sink_ref, q_ref, kc_ref, vc_ref, o_ref):
    q = q_ref[0]
    for g in range(A_KV_HEADS):
        sl = slice(g * HEAD_DIM, (g + 1) * HEAD_DIM)
        _gqa_heads(q, kc_ref[0, :, sl], vc_ref[0, :, sl], None, sink_ref, g, o_ref)


def _ctx_attention(qc, kc, vc, sink):
    b, c, dq = qc.shape
    dk = kc.shape[-1]
    est = 4 * c * dq * 2 + 4 * c * dk * 2 + 6 * A_GROUP * c * c * 4
    return pl.pallas_call(
        _ctx_attn_kernel,
        grid=(b,),
        in_specs=[
            pl.BlockSpec(memory_space=pltpu.SMEM),
            pl.BlockSpec((1, c, dq), lambda bi: (bi, 0, 0)),
            pl.BlockSpec((1, c, dk), lambda bi: (bi, 0, 0)),
            pl.BlockSpec((1, c, dk), lambda bi: (bi, 0, 0)),
        ],
        out_specs=pl.BlockSpec((1, c, dq), lambda bi: (bi, 0, 0)),
        out_shape=jax.ShapeDtypeStruct((b, c, dq), BF16),
        compiler_params=_params(est, 1),
        name="ctx_gqa_attn",
    )(sink, qc, kc, vc)


def _diff_attn_kernel(lq1_ref, lk1_ref, lq2_ref, lk2_ref, g_ref, q_ref, k_ref, v_ref, kc_ref, vc_ref,
                      o_ref, s_even, s_odd, *, lam_init, tq):
    seq = k_ref.shape[1]
    n_tiles = seq // tq
    lam = (jnp.exp(jnp.sum(lq1_ref[...] * lk1_ref[...], axis=-1, keepdims=True))
           - jnp.exp(jnp.sum(lq2_ref[...] * lk2_ref[...], axis=-1, keepdims=True)) + lam_init)
    lane = lax.broadcasted_iota(jnp.int32, (tq, 2 * HEAD_DIM), 1)
    contract_last = (((1,), (1,)), ((), ()))
    out_gain = g_ref[...] * (1.0 - lam_init)

    def scores(i, s_ref):
        q = q_ref[0, i * tq:(i + 1) * tq, :]
        zero = jnp.zeros_like(q)
        qq = jnp.concatenate([jnp.where(lane < HEAD_DIM, q, zero), jnp.where(lane >= HEAD_DIM, q, zero)], axis=0)
        s_ref[:, :seq] = lax.dot_general(qq, k_ref[0], contract_last, preferred_element_type=F32)
        s_ref[:, seq:] = lax.dot_general(qq, kc_ref[0], contract_last, preferred_element_type=F32)

    def finish(i, s_ref):
        s = s_ref[...]
        p = jnp.exp2(s - jnp.max(s, axis=-1, keepdims=True))
        norm = jnp.sum(p, axis=-1, keepdims=True)
        n0, n1 = norm[:tq], norm[tq:]
        a = (p[:tq] - p[tq:] * (lam * n0 / n1)).astype(BF16)
        o = (jnp.dot(a[:, :seq], v_ref[0], preferred_element_type=F32)
             + jnp.dot(a[:, seq:], vc_ref[0], preferred_element_type=F32)) / n0
        o = o * lax.rsqrt(jnp.mean(o * o, axis=-1, keepdims=True) + SUBLN_EPS)
        o_ref[0, i * tq:(i + 1) * tq, :] = (o * out_gain).astype(BF16)

    bufs = (s_even, s_odd)
    scores(0, bufs[0])
    for i in range(n_tiles):
        if i + 1 < n_tiles:
            scores(i + 1, bufs[(i + 1) % 2])
        finish(i, bufs[i % 2])


def _diff_attention(q, k, v, kc, vc, lq1, lk1, lq2, lk2, subln_g, lam_init, tq):
    b, l, dm = q.shape
    c = kc.shape[1]
    hd = 2 * HEAD_DIM
    vec = lambda n: pl.BlockSpec((1, n), lambda bi, h: (0, 0))
    per_head = lambda rows: pl.BlockSpec((1, rows, hd), lambda bi, h: (bi, 0, h))
    score_buf = pltpu.VMEM((2 * tq, l + c), F32)
    est = 2 * (3 * l + 2 * c) * hd * 2 * 2 + 6 * 2 * tq * (l + c) * 4
    return pl.pallas_call(
        functools.partial(_diff_attn_kernel, lam_init=lam_init, tq=tq),
        grid=(b, dm // hd),
        in_specs=[
            vec(HEAD_DIM), vec(HEAD_DIM), vec(HEAD_DIM), vec(HEAD_DIM), vec(hd),
            per_head(l), per_head(l), per_head(l), per_head(c), per_head(c),
        ],
        out_specs=per_head(l),
        out_shape=jax.ShapeDtypeStruct((b, l, dm), BF16),
        scratch_shapes=[score_buf, score_buf],
        compiler_params=_params(est, 2),
        name="diff_attn",
    )(lq1, lk1, lq2, lk2, subln_g, q, k, v, kc, vc)


def _post_kernel(a_ref, x_ref, mod_ref, wo_ref, w1_ref, w2_ref, ln_ref, o_ref, *, ff_chunk):
    d = x_ref.shape[-1]
    gate1 = mod_ref[0, :, 2 * d:3 * d]
    shift2 = mod_ref[0, :, 3 * d:4 * d]
    scale2 = mod_ref[0, :, 4 * d:5 * d]
    gate2 = mod_ref[0, :, 5 * d:6 * d]
    ax = jnp.dot(a_ref[0], wo_ref[...], preferred_element_type=F32)
    x1 = _layer_norm(DEEPNORM_ALPHA * x_ref[0] + gate1 * ax, ln_ref[0:1, :], ln_ref[1:2, :])
    h = (x1 * (1.0 + scale2) + shift2).astype(BF16)
    fx = jnp.zeros_like(x1)
    for j in range(w1_ref.shape[1] // ff_chunk):
        u = jnp.dot(h, w1_ref[:, j * ff_chunk:(j + 1) * ff_chunk], preferred_element_type=F32)
        u = jnp.maximum(u, 0.0)
        fx = fx + jnp.dot((u * u).astype(BF16), w2_ref[j * ff_chunk:(j + 1) * ff_chunk, :],
                          preferred_element_type=F32)
    o_ref[0] = _layer_norm(DEEPNORM_ALPHA * x1 + gate2 * fx, ln_ref[2:3, :], ln_ref[3:4, :])


def _post(attn, x, mod, wo, w1, w2, ln, tm):
    nb, rows, d = x.shape
    dff = w1.shape[1]
    ff_chunk = 1024
    mod_map = (lambda b, i: (b, 0, 0)) if mod.shape[0] == nb and nb > 1 else (lambda b, i: (0, 0, 0))
    const = lambda shape: pl.BlockSpec(shape, lambda b, i: (0, 0), pipeline_mode=pl.Buffered(1))
    est = ((d * d + 2 * d * dff) * 2 + 2 * tm * d * (2 + 4 + 4)
           + tm * d * 4 * 4 + tm * ff_chunk * 6)
    return pl.pallas_call(
        functools.partial(_post_kernel, ff_chunk=ff_chunk),
        grid=(nb, rows // tm),
        in_specs=[
            pl.BlockSpec((1, tm, d), lambda b, i: (b, i, 0)),
            pl.BlockSpec((1, tm, d), lambda b, i: (b, i, 0)),
            pl.BlockSpec((1, 1, mod.shape[-1]), mod_map),
            const(wo.shape), const(w1.shape), const(w2.shape),
            pl.BlockSpec(ln.shape, lambda b, i: (0, 0)),
        ],
        out_specs=pl.BlockSpec((1, tm, d), lambda b, i: (b, i, 0)),
        out_shape=jax.ShapeDtypeStruct((nb, rows, d), F32),
        compiler_params=_params(est, 2),
        name="oproj_ln_mlp_ln",
    )(attn, x, mod, wo, w1, w2, ln)


def _rope_tables(seq):
    rows = seq // GRID_W
    row = jnp.repeat(jnp.arange(rows, dtype=F32), GRID_W)
    col = jnp.tile(jnp.arange(GRID_W, dtype=F32), rows)
    n_freq = HEAD_DIM // 4
    inv = ROPE_BASE ** (-jnp.arange(n_freq, dtype=F32) / n_freq)
    ang = jnp.concatenate([row[:, None] * inv, col[:, None] * inv], axis=-1)
    cos, sin = jnp.cos(ang), jnp.sin(ang)
    reps = V7X_LANES // HEAD_DIM
    cos_t = jnp.tile(jnp.concatenate([cos, cos], axis=-1), (1, reps))
    sin_t = jnp.tile(jnp.concatenate([-sin, sin], axis=-1), (1, reps))
    return cos_t, sin_t


def kernel(x, c, ctx, c_ctx, w_ada, b_ada, ln1_g, ln1_b, ln2_g, ln2_b, a_wq, a_wk, a_wv, a_wo, a_sink,
           b_wq, b_wk, b_wv, b_wo, b_lq1, b_lk1, b_lq2, b_lk2, b_subln_g, mlp_w1, mlp_w2):
    batch, seq, d = x.shape
    ctx_len = ctx.shape[1]
    tables = _rope_tables(seq)
    bf = lambda w: w.astype(BF16)

    cond = jnp.concatenate([c, c_ctx[None, :], jnp.zeros((MOD_ROWS - batch - 1, d), F32)], axis=0)
    mod = _ada(cond, w_ada, b_ada[:, None, :])

    for i in range(DEPTH):
        need_ctx = i < DEPTH - 1
        mod_x = mod[i, :batch][:, None, :]
        mod_c = mod[i, batch:batch + 1][:, None, :]
        ln = jnp.stack([ln1_g[i], ln1_b[i], ln2_g[i], ln2_b[i]], axis=0)
        j = i // N_MIXERS
        if i % N_MIXERS == 0:
            wq, wk, wv, wo = bf(a_wq[j]), bf(a_wk[j]), bf(a_wv[j]), bf(a_wo[j])
            q, k, v = _project(x, mod_x, [wq, wk, wv], tables, [QK_SCALE, 1.0, None], tm=512)
            if need_ctx:
                qc, kc, vc = _project(ctx, mod_c, [wq, wk, wv], None, [QK_SCALE, None, None], tm=ctx_len)
            else:
                kc, vc = _project(ctx, mod_c, [wk, wv], None, [None, None], tm=ctx_len)
            attn_x = _window_attention(q, k, v, kc, vc, a_sink[j])
            if need_ctx:
                attn_c = _ctx_attention(qc, kc, vc, a_sink[j])
        else:
            lam_init = 0.8 - 0.6 * math.exp(-0.3 * i)
            wq, wk, wv, wo = bf(b_wq[j]), bf(b_wk[j]), bf(b_wv[j]), bf(b_wo[j])
            q, k, v = _project(x, mod_x, [wq, wk, wv], tables, [QK_SCALE * LOG2_E, 1.0, None], tm=512)
            if need_ctx:
                raise NotImplementedError("context queries through differential attention")
            kc, vc = _project(ctx, mod_c, [wk, wv], None, [None, None], tm=ctx_len)
            attn_x = _diff_attention(q, k, v, kc, vc, b_lq1[j][None], b_lk1[j][None], b_lq2[j][None],
                                     b_lk2[j][None], b_subln_g[j][None], lam_init, tq=256)
        w1, w2 = bf(mlp_w1[i]), bf(mlp_w2[i])
        x = _post(attn_x, x, mod_x, wo, w1, w2, ln, tm=512)
        if need_ctx:
            ctx = _post(attn_c, ctx, mod_c, wo, w1, w2, ln, tm=ctx_len)
    return x
```

```python
import functools
import math

import jax
import jax.numpy as jnp
from jax import lax
from jax.experimental import pallas as pl
from jax.experimental.pallas import tpu as pltpu

D_MODEL = 1024
DEPTH = 2
GRID_W = 64
N_MIXERS = 2
BLOCK = 128
HEAD_DIM = 64
A_HEADS = D_MODEL // HEAD_DIM
A_KV_HEADS = max(1, A_HEADS // 8)
A_GROUP = A_HEADS // A_KV_HEADS
B_HEADS = D_MODEL // (2 * HEAD_DIM)
D_FF = 4 * D_MODEL
N_MOD = 6
ROPE_BASE = 10000.0
LN_EPS = 1e-5
SUBLN_EPS = 1e-5
NEG_INF = -1e30
DEEPNORM_ALPHA = (2 * DEPTH) ** 0.25
QK_SCALE = HEAD_DIM ** -0.5
LOG2_E = math.log2(math.e)

V7X_LANES = 128
V7X_VMEM_BYTES = 64 * 1024 * 1024
MOD_ROWS = 24

BF16 = jnp.bfloat16
F32 = jnp.float32


def _vmem_limit(estimate_bytes):
    return int(min(max(2 * estimate_bytes, 16 * 1024 * 1024), V7X_VMEM_BYTES - 8 * 1024 * 1024))


def _params(estimate_bytes, n_grid):
    return pltpu.CompilerParams(
        dimension_semantics=("arbitrary",) * n_grid,
        vmem_limit_bytes=_vmem_limit(estimate_bytes),
    )


def _layer_norm(y, g, b):
    mu = jnp.mean(y, axis=-1, keepdims=True)
    yc = y - mu
    var = jnp.mean(yc * yc, axis=-1, keepdims=True)
    return yc * lax.rsqrt(var + LN_EPS) * g + b


def _ada_kernel(c_ref, w_ref, b_ref, o_ref):
    c = c_ref[...]
    h = (c * jax.nn.sigmoid(c)).astype(BF16)
    acc = jnp.dot(h, w_ref[0].astype(BF16), preferred_element_type=F32)
    o_ref[0] = acc + b_ref[0]


def _ada(cond, w_ada, b_ada):
    depth, d, n = w_ada.shape
    tn = 1024
    est = 2 * d * tn * 4 + d * tn * 2 + MOD_ROWS * (d + 2 * tn) * 4
    return pl.pallas_call(
        _ada_kernel,
        grid=(depth, n // tn),
        in_specs=[
            pl.BlockSpec((MOD_ROWS, d), lambda i, j: (0, 0)),
            pl.BlockSpec((1, d, tn), lambda i, j: (i, 0, j)),
            pl.BlockSpec((1, 1, tn), lambda i, j: (i, 0, j)),
        ],
        out_specs=pl.BlockSpec((1, MOD_ROWS, tn), lambda i, j: (i, 0, j)),
        out_shape=jax.ShapeDtypeStruct((depth, MOD_ROWS, n), F32),
        compiler_params=_params(est, 2),
        name="ada_mod",
    )(cond, w_ada, b_ada)


def _rope(y, cos, sin_signed, first_half):
    half = HEAD_DIM // 2
    outs = []
    for g in range(y.shape[1] // V7X_LANES):
        yg = y[:, g * V7X_LANES:(g + 1) * V7X_LANES]
        partner = jnp.where(first_half,
                            pltpu.roll(yg, V7X_LANES - half, 1),
                            pltpu.roll(yg, half, 1))
        outs.append(yg * cos + partner * sin_signed)
    return outs


def _proj_kernel(*refs, n_out, rope, scales):
    x_ref, mod_ref = refs[0], refs[1]
    w_refs = refs[2:2 + n_out]
    pos = 2 + n_out
    if rope:
        cos_ref, sin_ref = refs[pos], refs[pos + 1]
        pos += 2
    o_refs = refs[pos:pos + n_out]
    d = x_ref.shape[-1]
    shift = mod_ref[0, :, 0:d]
    scale = mod_ref[0, :, d:2 * d]
    h = (x_ref[0] * (1.0 + scale) + shift).astype(BF16)
    if rope:
        cos = cos_ref[...]
        sin_signed = sin_ref[...]
        lane = lax.broadcasted_iota(jnp.int32, cos.shape, 1)
        first_half = (lane % HEAD_DIM) < (HEAD_DIM // 2)
    for j in range(n_out):
        y = jnp.dot(h, w_refs[j][...], preferred_element_type=F32)
        roped = rope and scales[j] is not None
        if roped:
            parts = _rope(y, cos, sin_signed, first_half)
            for g, part in enumerate(parts):
                if scales[j] != 1.0:
                    part = part * scales[j]
                o_refs[j][0, :, g * V7X_LANES:(g + 1) * V7X_LANES] = part.astype(BF16)
        else:
            if scales[j] is not None and scales[j] != 1.0:
                y = y * scales[j]
            o_refs[j][0] = y.astype(BF16)


def _project(x, mod, weights, rope_tables, scales, tm):
    nb, rows, d = x.shape
    n_out = len(weights)
    rope = rope_tables is not None
    mod_map = (lambda b, i: (b, 0, 0)) if mod.shape[0] == nb and nb > 1 else (lambda b, i: (0, 0, 0))
    in_specs = [
        pl.BlockSpec((1, tm, d), lambda b, i: (b, i, 0)),
        pl.BlockSpec((1, 1, mod.shape[-1]), mod_map),
    ]
    for w in weights:
        in_specs.append(pl.BlockSpec(w.shape, lambda b, i: (0, 0)))
    args = [x, mod] + list(weights)
    if rope:
        in_specs += [pl.BlockSpec((tm, V7X_LANES), lambda b, i: (i, 0))] * 2
        args += list(rope_tables)
    n_tot = sum(w.shape[1] for w in weights)
    est = 2 * tm * d * 4 + 2 * d * n_tot * 2 + 2 * tm * n_tot * 2 + tm * d * 2 + 2 * tm * n_tot * 4
    return pl.pallas_call(
        functools.partial(_proj_kernel, n_out=n_out, rope=rope, scales=tuple(scales)),
        grid=(nb, rows // tm),
        in_specs=in_specs,
        out_specs=[pl.BlockSpec((1, tm, w.shape[1]), lambda b, i: (b, i, 0)) for w in weights],
        out_shape=[jax.ShapeDtypeStruct((nb, rows, w.shape[1]), BF16) for w in weights],
        compiler_params=_params(est, 2),
        name="mod_proj_rope" if rope else "mod_proj",
    )(*args)


def _gqa_block(q, keys, vals, valid_t, sink_ref, o_ref):
    tq = q.shape[0]
    width = A_KV_HEADS * HEAD_DIM
    contract_last = (((1,), (1,)), ((), ()))
    qs = jnp.concatenate([q[:, h * width:(h + 1) * width] for h in range(A_GROUP)], axis=0)
    q_lane = lax.broadcasted_iota(jnp.int32, qs.shape, 1)
    vals_t = vals.astype(F32).T.astype(BF16)
    if valid_t is not None:
        valid_t = jnp.concatenate([valid_t] * A_GROUP, axis=1)
    outs = []
    for g in range(A_KV_HEADS):
        qg = jnp.where((q_lane >= g * HEAD_DIM) & (q_lane < (g + 1) * HEAD_DIM), qs, jnp.zeros_like(qs))
        s = lax.dot_general(keys, qg, contract_last, preferred_element_type=F32)
        if valid_t is not None:
            s = jnp.where(valid_t, s, NEG_INF)
        sink = jnp.concatenate(
            [jnp.full((1, tq), sink_ref[g * A_GROUP + h] * LOG2_E, F32) for h in range(A_GROUP)], axis=1)
        m = jnp.maximum(jnp.max(s, axis=0, keepdims=True), sink)
        p = jnp.exp2(s - m)
        inv = 1.0 / (jnp.sum(p, axis=0, keepdims=True) + jnp.exp2(sink - m))
        og = jnp.dot(vals_t[g * HEAD_DIM:(g + 1) * HEAD_DIM, :], p.astype(BF16), preferred_element_type=F32)
        outs.append(og * inv)
    out_t = jnp.concatenate(outs, axis=0)
    for h in range(A_GROUP):
        o_ref[0, :, h * width:(h + 1) * width] = out_t[:, h * tq:(h + 1) * tq].T.astype(BF16)


def _window_attn_kernel(sink_ref, q_ref, kp_ref, kn_ref, kx_ref, vp_ref, vn_ref, vx_ref,
                        kc_ref, vc_ref, o_ref):
    n = pl.program_id(1)
    nblk = pl.num_programs(1)
    ctx_len = kc_ref.shape[1]
    key = lax.broadcasted_iota(jnp.int32, (BLOCK, BLOCK), 0)
    qry = lax.broadcasted_iota(jnp.int32, (BLOCK, BLOCK), 1)
    valid_t = jnp.concatenate([
        (key >= qry) & (n > 0),
        jnp.ones((BLOCK, BLOCK), jnp.bool_),
        (key <= qry) & (n < nblk - 1),
        jnp.ones((ctx_len, BLOCK), jnp.bool_),
    ], axis=0)
    keys = jnp.concatenate([kp_ref[0], kn_ref[0], kx_ref[0], kc_ref[0]], axis=0)
    vals = jnp.concatenate([vp_ref[0], vn_ref[0], vx_ref[0], vc_ref[0]], axis=0)
    _gqa_block(q_ref[0], keys, vals, valid_t, sink_ref, o_ref)


def _window_attention(q, k, v, kc, vc, sink):
    b, l, dq = q.shape
    dk = k.shape[-1]
    c = kc.shape[1]
    nblk = l // BLOCK
    prev_map = lambda bi, n: (bi, jnp.maximum(n - 1, 0), 0)
    cur_map = lambda bi, n: (bi, n, 0)
    next_map = lambda bi, n: (bi, jnp.minimum(n + 1, nblk - 1), 0)
    ctx_map = lambda bi, n: (bi, 0, 0)
    kv_spec = lambda m: pl.BlockSpec((1, BLOCK, dk), m)
    est = 4 * BLOCK * dq * 2 + 16 * BLOCK * dk * 2 + 6 * A_GROUP * BLOCK * (3 * BLOCK + c) * 4
    return pl.pallas_call(
        _window_attn_kernel,
        grid=(b, nblk),
        in_specs=[
            pl.BlockSpec(memory_space=pltpu.SMEM),
            pl.BlockSpec((1, BLOCK, dq), cur_map),
            kv_spec(prev_map), kv_spec(cur_map), kv_spec(next_map),
            kv_spec(prev_map), kv_spec(cur_map), kv_spec(next_map),
            pl.BlockSpec((1, c, dk), ctx_map),
            pl.BlockSpec((1, c, dk), ctx_map),
        ],
        out_specs=pl.BlockSpec((1, BLOCK, dq), cur_map),
        out_shape=jax.ShapeDtypeStruct((b, l, dq), BF16),
        compiler_params=_params(est, 2),
        name="window_gqa_attn",
    )(sink, q, k, k, k, v, v, v, kc, vc)


def _ctx_attn_kernel(sink_ref, q_ref, kc_ref, vc_ref, o_ref):
    _gqa_block(q_ref[0], kc_ref[0], vc_ref[0], None, sink_ref, o_ref)


def _ctx_attention(qc, kc, vc, sink):
    b, c, dq = qc.shape
    dk = kc.shape[-1]
    est = 4 * c * dq * 2 + 4 * c * dk * 2 + 6 * A_GROUP * c * c * 4
    return pl.pallas_call(
        _ctx_attn_kernel,
        grid=(b,),
        in_specs=[
            pl.BlockSpec(memory_space=pltpu.SMEM),
            pl.BlockSpec((1, c, dq), lambda bi: (bi, 0, 0)),
            pl.BlockSpec((1, c, dk), lambda bi: (bi, 0, 0)),
            pl.BlockSpec((1, c, dk), lambda bi: (bi, 0, 0)),
        ],
        out_specs=pl.BlockSpec((1, c, dq), lambda bi: (bi, 0, 0)),
        out_shape=jax.ShapeDtypeStruct((b, c, dq), BF16),
        compiler_params=_params(est, 1),
        name="ctx_gqa_attn",
    )(sink, qc, kc, vc)


def _diff_attn_kernel(lq1_ref, lk1_ref, lq2_ref, lk2_ref, g_ref, q_ref, k_ref, v_ref, kc_ref, vc_ref,
                      o_ref, k_all, v_all_t, s_even, s_odd, *, lam_init, tq):
    seq = k_ref.shape[1]
    n_tiles = seq // tq
    lam = (jnp.exp(jnp.sum(lq1_ref[...] * lk1_ref[...], axis=-1, keepdims=True))
           - jnp.exp(jnp.sum(lq2_ref[...] * lk2_ref[...], axis=-1, keepdims=True)) + lam_init)
    lane = lax.broadcasted_iota(jnp.int32, (tq, 2 * HEAD_DIM), 1)
    contract_last = (((1,), (1,)), ((), ()))
    out_gain = g_ref[...] * (1.0 - lam_init)

    k_all[:seq, :] = k_ref[0]
    k_all[seq:, :] = kc_ref[0]
    v_all_t[:, :seq] = v_ref[0].astype(F32).T.astype(BF16)
    v_all_t[:, seq:] = vc_ref[0].astype(F32).T.astype(BF16)

    def scores(i, s_ref):
        q = q_ref[0, i * tq:(i + 1) * tq, :]
        zero = jnp.zeros_like(q)
        qq = jnp.concatenate([jnp.where(lane < HEAD_DIM, q, zero), jnp.where(lane >= HEAD_DIM, q, zero)], axis=0)
        s_ref[...] = lax.dot_general(k_all[...], qq, contract_last, preferred_element_type=F32)

    def finish(i, s_ref):
        s = s_ref[...]
        p = jnp.exp2(s - jnp.max(s, axis=0, keepdims=True))
        norm = jnp.sum(p, axis=0, keepdims=True)
        n0, n1 = norm[:, :tq], norm[:, tq:]
        a_t = (p[:, :tq] - p[:, tq:] * (lam * n0 / n1)).astype(BF16)
        o_t = jnp.dot(v_all_t[...], a_t, preferred_element_type=F32) / n0
        o_t = o_t * lax.rsqrt(jnp.mean(o_t * o_t, axis=0, keepdims=True) + SUBLN_EPS)
        o_ref[0, i * tq:(i + 1) * tq, :] = (o_t * out_gain).T.astype(BF16)

    bufs = (s_even, s_odd)
    scores(0, bufs[0])
    for i in range(n_tiles):
        if i + 1 < n_tiles:
            scores(i + 1, bufs[(i + 1) % 2])
        finish(i, bufs[i % 2])


def _diff_attention(q, k, v, kc, vc, lq1, lk1, lq2, lk2, subln_g, lam_init, tq):
    b, l, dm = q.shape
    c = kc.shape[1]
    hd = 2 * HEAD_DIM
    vec = lambda n: pl.BlockSpec((1, n), lambda bi, h: (0, 0))
    per_head = lambda rows: pl.BlockSpec((1, rows, hd), lambda bi, h: (bi, 0, h))
    score_buf = pltpu.VMEM((l + c, 2 * tq), F32)
    est = 2 * (3 * l + 2 * c) * hd * 2 * 2 + 6 * 2 * tq * (l + c) * 4
    return pl.pallas_call(
        functools.partial(_diff_attn_kernel, lam_init=lam_init, tq=tq),
        grid=(b, dm // hd),
        in_specs=[
            vec(HEAD_DIM), vec(HEAD_DIM), vec(HEAD_DIM), vec(HEAD_DIM),
            pl.BlockSpec((hd, 1), lambda bi, h: (0, 0)),
            per_head(l), per_head(l), per_head(l), per_head(c), per_head(c),
        ],
        out_specs=per_head(l),
        out_shape=jax.ShapeDtypeStruct((b, l, dm), BF16),
        scratch_shapes=[pltpu.VMEM((l + c, hd), BF16), pltpu.VMEM((hd, l + c), BF16), score_buf, score_buf],
        compiler_params=_params(est, 2),
        name="diff_attn",
    )(lq1, lk1, lq2, lk2, subln_g, q, k, v, kc, vc)


def _post_kernel(a_ref, x_ref, mod_ref, wo_ref, w1_ref, w2_ref, ln_ref, o_ref, *, ff_chunk):
    d = x_ref.shape[-1]
    gate1 = mod_ref[0, :, 2 * d:3 * d]
    shift2 = mod_ref[0, :, 3 * d:4 * d]
    scale2 = mod_ref[0, :, 4 * d:5 * d]
    gate2 = mod_ref[0, :, 5 * d:6 * d]
    ax = jnp.dot(a_ref[0], wo_ref[...], preferred_element_type=F32)
    x1 = _layer_norm(DEEPNORM_ALPHA * x_ref[0] + gate1 * ax, ln_ref[0:1, :], ln_ref[1:2, :])
    h = (x1 * (1.0 + scale2) + shift2).astype(BF16)
    fx = jnp.zeros_like(x1)
    for j in range(w1_ref.shape[1] // ff_chunk):
        u = jnp.dot(h, w1_ref[:, j * ff_chunk:(j + 1) * ff_chunk], preferred_element_type=F32)
        u = jnp.maximum(u, 0.0)
        fx = fx + jnp.dot((u * u).astype(BF16), w2_ref[j * ff_chunk:(j + 1) * ff_chunk, :],
                          preferred_element_type=F32)
    o_ref[0] = _layer_norm(DEEPNORM_ALPHA * x1 + gate2 * fx, ln_ref[2:3, :], ln_ref[3:4, :])


def _post(attn, x, mod, wo, w1, w2, ln, tm):
    nb, rows, d = x.shape
    dff = w1.shape[1]
    ff_chunk = 1024
    mod_map = (lambda b, i: (b, 0, 0)) if mod.shape[0] == nb and nb > 1 else (lambda b, i: (0, 0, 0))
    const = lambda shape: pl.BlockSpec(shape, lambda b, i: (0, 0), pipeline_mode=pl.Buffered(1))
    est = ((d * d + 2 * d * dff) * 2 + 2 * tm * d * (2 + 4 + 4)
           + tm * d * 4 * 4 + tm * ff_chunk * 6)
    return pl.pallas_call(
        functools.partial(_post_kernel, ff_chunk=ff_chunk),
        grid=(nb, rows // tm),
        in_specs=[
            pl.BlockSpec((1, tm, d), lambda b, i: (b, i, 0)),
            pl.BlockSpec((1, tm, d), lambda b, i: (b, i, 0)),
            pl.BlockSpec((1, 1, mod.shape[-1]), mod_map),
            const(wo.shape), const(w1.shape), const(w2.shape),
            pl.BlockSpec(ln.shape, lambda b, i: (0, 0)),
        ],
        out_specs=pl.BlockSpec((1, tm, d), lambda b, i: (b, i, 0)),
        out_shape=jax.ShapeDtypeStruct((nb, rows, d), F32),
        compiler_params=_params(est, 2),
        name="oproj_ln_mlp_ln",
    )(attn, x, mod, wo, w1, w2, ln)


def _rope_tables(seq):
    rows = seq // GRID_W
    row = jnp.repeat(jnp.arange(rows, dtype=F32), GRID_W)
    col = jnp.tile(jnp.arange(GRID_W, dtype=F32), rows)
    n_freq = HEAD_DIM // 4
    inv = ROPE_BASE ** (-jnp.arange(n_freq, dtype=F32) / n_freq)
    ang = jnp.concatenate([row[:, None] * inv, col[:, None] * inv], axis=-1)
    cos, sin = jnp.cos(ang), jnp.sin(ang)
    reps = V7X_LANES // HEAD_DIM
    cos_t = jnp.tile(jnp.concatenate([cos, cos], axis=-1), (1, reps))
    sin_t = jnp.tile(jnp.concatenate([-sin, sin], axis=-1), (1, reps))
    return cos_t, sin_t


def kernel(x, c, ctx, c_ctx, w_ada, b_ada, ln1_g, ln1_b, ln2_g, ln2_b, a_wq, a_wk, a_wv, a_wo, a_sink,
           b_wq, b_wk, b_wv, b_wo, b_lq1, b_lk1, b_lq2, b_lk2, b_subln_g, mlp_w1, mlp_w2):
    batch, seq, d = x.shape
    ctx_len = ctx.shape[1]
    tables = _rope_tables(seq)
    bf = lambda w: w.astype(BF16)

    cond = jnp.concatenate([c, c_ctx[None, :], jnp.zeros((MOD_ROWS - batch - 1, d), F32)], axis=0)
    mod = _ada(cond, w_ada, b_ada[:, None, :])

    for i in range(DEPTH):
        need_ctx = i < DEPTH - 1
        mod_x = mod[i, :batch][:, None, :]
        mod_c = mod[i, batch:batch + 1][:, None, :]
        ln = jnp.stack([ln1_g[i], ln1_b[i], ln2_g[i], ln2_b[i]], axis=0)
        j = i // N_MIXERS
        if i % N_MIXERS == 0:
            wq = a_wq[j].reshape(d, A_KV_HEADS, A_GROUP, HEAD_DIM).swapaxes(1, 2).reshape(d, d)
            wo = a_wo[j].reshape(A_KV_HEADS, A_GROUP, HEAD_DIM, d).swapaxes(0, 1).reshape(d, d)
            wq, wk, wv, wo = bf(wq), bf(a_wk[j]), bf(a_wv[j]), bf(wo)
            q, k, v = _project(x, mod_x, [wq, wk, wv], tables, [QK_SCALE * LOG2_E, 1.0, None], tm=512)
            if need_ctx:
                qc, kc, vc = _project(ctx, mod_c, [wq, wk, wv], None, [QK_SCALE * LOG2_E, None, None],
                                      tm=ctx_len)
            else:
                kc, vc = _project(ctx, mod_c, [wk, wv], None, [None, None], tm=ctx_len)
            attn_x = _window_attention(q, k, v, kc, vc, a_sink[j])
            if need_ctx:
                attn_c = _ctx_attention(qc, kc, vc, a_sink[j])
        else:
            lam_init = 0.8 - 0.6 * math.exp(-0.3 * i)
            wq, wk, wv, wo = bf(b_wq[j]), bf(b_wk[j]), bf(b_wv[j]), bf(b_wo[j])
            q, k, v = _project(x, mod_x, [wq, wk, wv], tables, [QK_SCALE * LOG2_E, 1.0, None], tm=512)
            if need_ctx:
                raise NotImplementedError("context queries through differential attention")
            kc, vc = _project(ctx, mod_c, [wk, wv], None, [None, None], tm=ctx_len)
            attn_x = _diff_attention(q, k, v, kc, vc, b_lq1[j][None], b_lk1[j][None], b_lq2[j][None],
                                     b_lk2[j][None], b_subln_g[j][:, None], lam_init, tq=256)
        w1, w2 = bf(mlp_w1[i]), bf(mlp_w2[i])
        x = _post(attn_x, x, mod_x, wo, w1, w2, ln, tm=512)
        if need_ctx:
            ctx = _post(attn_c, ctx, mod_c, wo, w1, w2, ln, tm=ctx_len)
    return x
```

```python
import functools
import math

import jax
import jax.numpy as jnp
from jax import lax
from jax.experimental import pallas as pl
from jax.experimental.pallas import tpu as pltpu

D_MODEL = 1024
DEPTH = 2
GRID_W = 64
N_MIXERS = 2
BLOCK = 128
HEAD_DIM = 64
A_HEADS = D_MODEL // HEAD_DIM
A_KV_HEADS = max(1, A_HEADS // 8)
A_GROUP = A_HEADS // A_KV_HEADS
B_HEADS = D_MODEL // (2 * HEAD_DIM)
D_FF = 4 * D_MODEL
N_MOD = 6
ROPE_BASE = 10000.0
LN_EPS = 1e-5
SUBLN_EPS = 1e-5
NEG_INF = -1e30
DEEPNORM_ALPHA = (2 * DEPTH) ** 0.25
QK_SCALE = HEAD_DIM ** -0.5
LOG2_E = math.log2(math.e)

V7X_LANES = 128
BF16_SUBLANES = 16
V7X_VMEM_BYTES = 64 * 1024 * 1024
MOD_ROWS = 24

BF16 = jnp.bfloat16
F32 = jnp.float32


def _vmem_limit(estimate_bytes):
    return int(min(max(2 * estimate_bytes, 16 * 1024 * 1024), V7X_VMEM_BYTES - 8 * 1024 * 1024))


def _params(estimate_bytes, n_grid, flags=None):
    return pltpu.CompilerParams(
        dimension_semantics=("arbitrary",) * n_grid,
        vmem_limit_bytes=_vmem_limit(estimate_bytes),
        flags=flags,
    )


def _layer_norm(y, g, b):
    mu = jnp.mean(y, axis=-1, keepdims=True)
    yc = y - mu
    var = jnp.mean(yc * yc, axis=-1, keepdims=True)
    return yc * lax.rsqrt(var + LN_EPS) * g + b


def _ada_kernel(c_ref, w_ref, b_ref, o_ref):
    c = c_ref[...]
    h = (c * jax.nn.sigmoid(c)).astype(BF16)
    acc = jnp.dot(h, w_ref[0].astype(BF16), preferred_element_type=F32)
    o_ref[0] = acc + b_ref[0]


def _ada(cond, w_ada, b_ada):
    depth, d, n = w_ada.shape
    tn = 1024
    est = 2 * d * tn * 4 + d * tn * 2 + MOD_ROWS * (d + 2 * tn) * 4
    return pl.pallas_call(
        _ada_kernel,
        grid=(depth, n // tn),
        in_specs=[
            pl.BlockSpec((MOD_ROWS, d), lambda i, j: (0, 0)),
            pl.BlockSpec((1, d, tn), lambda i, j: (i, 0, j)),
            pl.BlockSpec((1, 1, tn), lambda i, j: (i, 0, j)),
        ],
        out_specs=pl.BlockSpec((1, MOD_ROWS, tn), lambda i, j: (i, 0, j)),
        out_shape=jax.ShapeDtypeStruct((depth, MOD_ROWS, n), F32),
        compiler_params=_params(est, 2),
        name="ada_mod",
    )(cond, w_ada, b_ada)


def _rope(y, cos, sin_signed, first_half):
    half = HEAD_DIM // 2
    outs = []
    for g in range(y.shape[1] // V7X_LANES):
        yg = y[:, g * V7X_LANES:(g + 1) * V7X_LANES]
        partner = jnp.where(first_half,
                            pltpu.roll(yg, V7X_LANES - half, 1),
                            pltpu.roll(yg, half, 1))
        outs.append(yg * cos + partner * sin_signed)
    return outs


def _proj_kernel(*refs, n_out, rope, scales):
    x_ref, mod_ref = refs[0], refs[1]
    w_refs = refs[2:2 + n_out]
    pos = 2 + n_out
    if rope:
        cos_ref, sin_ref = refs[pos], refs[pos + 1]
        pos += 2
    o_refs = refs[pos:pos + n_out]
    d = x_ref.shape[-1]
    shift = mod_ref[0, :, 0:d]
    scale = mod_ref[0, :, d:2 * d]
    h = (x_ref[0] * (1.0 + scale) + shift).astype(BF16)
    if rope:
        cos = cos_ref[...]
        sin_signed = sin_ref[...]
        lane = lax.broadcasted_iota(jnp.int32, cos.shape, 1)
        first_half = (lane % HEAD_DIM) < (HEAD_DIM // 2)
    for j in range(n_out):
        y = jnp.dot(h, w_refs[j][...], preferred_element_type=F32)
        roped = rope and scales[j] is not None
        if roped:
            parts = _rope(y, cos, sin_signed, first_half)
            for g, part in enumerate(parts):
                if scales[j] != 1.0:
                    part = part * scales[j]
                o_refs[j][0, :, g * V7X_LANES:(g + 1) * V7X_LANES] = part.astype(BF16)
        else:
            if scales[j] is not None and scales[j] != 1.0:
                y = y * scales[j]
            o_refs[j][0] = y.astype(BF16)


def _project(x, mod, weights, rope_tables, scales, tm):
    nb, rows, d = x.shape
    n_out = len(weights)
    rope = rope_tables is not None
    mod_map = (lambda b, i: (b, 0, 0)) if mod.shape[0] == nb and nb > 1 else (lambda b, i: (0, 0, 0))
    in_specs = [
        pl.BlockSpec((1, tm, d), lambda b, i: (b, i, 0)),
        pl.BlockSpec((1, 1, mod.shape[-1]), mod_map),
    ]
    for w in weights:
        in_specs.append(pl.BlockSpec(w.shape, lambda b, i: (0, 0)))
    args = [x, mod] + list(weights)
    if rope:
        in_specs += [pl.BlockSpec((tm, V7X_LANES), lambda b, i: (i, 0))] * 2
        args += list(rope_tables)
    n_tot = sum(w.shape[1] for w in weights)
    est = 2 * tm * d * 4 + 2 * d * n_tot * 2 + 2 * tm * n_tot * 2 + tm * d * 2 + 2 * tm * n_tot * 4
    return pl.pallas_call(
        functools.partial(_proj_kernel, n_out=n_out, rope=rope, scales=tuple(scales)),
        grid=(nb, rows // tm),
        in_specs=in_specs,
        out_specs=[pl.BlockSpec((1, tm, w.shape[1]), lambda b, i: (b, i, 0)) for w in weights],
        out_shape=[jax.ShapeDtypeStruct((nb, rows, w.shape[1]), BF16) for w in weights],
        compiler_params=_params(est, 2),
        name="mod_proj_rope" if rope else "mod_proj",
    )(*args)


def _gqa_block(q, keys, vals, masks, sink_ref, o_ref):
    tq = q.shape[0]
    nk = keys.shape[0]
    width = A_KV_HEADS * HEAD_DIM
    contract_last = (((1,), (1,)), ((), ()))
    qs = jnp.concatenate([q[:, h * width:(h + 1) * width] for h in range(A_GROUP)], axis=0)
    q_lane = lax.broadcasted_iota(jnp.int32, qs.shape, 1)
    vals_t = vals.astype(F32).T.astype(BF16)
    pad_row = lax.broadcasted_iota(jnp.int32, (BF16_SUBLANES, nk), 0)
    ones_rows = jnp.where(pad_row == 0, 1.0, 0.0).astype(BF16)
    masks = [(r0, jnp.concatenate([valid] * A_GROUP, axis=1)) for r0, valid in masks]
    outs = []
    for g in range(A_KV_HEADS):
        qg = jnp.where((q_lane >= g * HEAD_DIM) & (q_lane < (g + 1) * HEAD_DIM), qs, jnp.zeros_like(qs))
        s = lax.dot_general(keys, qg, contract_last, preferred_element_type=F32)
        pieces, row = [], 0
        for r0, valid in masks:
            if r0 > row:
                pieces.append(s[row:r0])
            pieces.append(jnp.where(valid, s[r0:r0 + valid.shape[0]], NEG_INF))
            row = r0 + valid.shape[0]
        if masks:
            s = jnp.concatenate(pieces + [s[row:]], axis=0)
        sink = jnp.concatenate(
            [jnp.full((1, tq), sink_ref[g * A_GROUP + h] * LOG2_E, F32) for h in range(A_GROUP)], axis=1)
        m = jnp.maximum(jnp.max(s, axis=0, keepdims=True), sink)
        p = jnp.exp2(s - m).astype(BF16)
        v_ext = jnp.concatenate([vals_t[g * HEAD_DIM:(g + 1) * HEAD_DIM, :], ones_rows], axis=0)
        acc = jnp.dot(v_ext, p, preferred_element_type=F32)
        inv = 1.0 / (acc[HEAD_DIM:HEAD_DIM + 1, :] + jnp.exp2(sink - m))
        outs.append(acc[:HEAD_DIM, :] * inv)
    out_t = jnp.concatenate(outs, axis=0)
    for h in range(A_GROUP):
        o_ref[0, :, h * width:(h + 1) * width] = out_t[:, h * tq:(h + 1) * tq].T.astype(BF16)


def _window_attn_kernel(sink_ref, q_ref, kp_ref, kn_ref, kx_ref, vp_ref, vn_ref, vx_ref,
                        kc_ref, vc_ref, o_ref):
    n = pl.program_id(1)
    nblk = pl.num_programs(1)
    key = lax.broadcasted_iota(jnp.int32, (BLOCK, BLOCK), 0)
    qry = lax.broadcasted_iota(jnp.int32, (BLOCK, BLOCK), 1)
    masks = [
        (0, (key >= qry) & (n > 0)),
        (2 * BLOCK, (key <= qry) & (n < nblk - 1)),
    ]
    keys = jnp.concatenate([kp_ref[0], kn_ref[0], kx_ref[0], kc_ref[0]], axis=0)
    vals = jnp.concatenate([vp_ref[0], vn_ref[0], vx_ref[0], vc_ref[0]], axis=0)
    _gqa_block(q_ref[0], keys, vals, masks, sink_ref, o_ref)


def _window_attention(q, k, v, kc, vc, sink):
    b, l, dq = q.shape
    dk = k.shape[-1]
    c = kc.shape[1]
    nblk = l // BLOCK
    prev_map = lambda bi, n: (bi, jnp.maximum(n - 1, 0), 0)
    cur_map = lambda bi, n: (bi, n, 0)
    next_map = lambda bi, n: (bi, jnp.minimum(n + 1, nblk - 1), 0)
    ctx_map = lambda bi, n: (bi, 0, 0)
    kv_spec = lambda m: pl.BlockSpec((1, BLOCK, dk), m)
    est = 4 * BLOCK * dq * 2 + 16 * BLOCK * dk * 2 + 6 * A_GROUP * BLOCK * (3 * BLOCK + c) * 4
    return pl.pallas_call(
        _window_attn_kernel,
        grid=(b, nblk),
        in_specs=[
            pl.BlockSpec(memory_space=pltpu.SMEM),
            pl.BlockSpec((1, BLOCK, dq), cur_map),
            kv_spec(prev_map), kv_spec(cur_map), kv_spec(next_map),
            kv_spec(prev_map), kv_spec(cur_map), kv_spec(next_map),
            pl.BlockSpec((1, c, dk), ctx_map),
            pl.BlockSpec((1, c, dk), ctx_map),
        ],
        out_specs=pl.BlockSpec((1, BLOCK, dq), cur_map),
        out_shape=jax.ShapeDtypeStruct((b, l, dq), BF16),
        compiler_params=_params(est, 2),
        name="window_gqa_attn",
    )(sink, q, k, k, k, v, v, v, kc, vc)


def _ctx_attn_kernel(sink_ref, q_ref, kc_ref, vc_ref, o_ref):
    _gqa_block(q_ref[0], kc_ref[0], vc_ref[0], [], sink_ref, o_ref)


def _ctx_attention(qc, kc, vc, sink):
    b, c, dq = qc.shape
    dk = kc.shape[-1]
    est = 4 * c * dq * 2 + 4 * c * dk * 2 + 6 * A_GROUP * c * c * 4
    return pl.pallas_call(
        _ctx_attn_kernel,
        grid=(b,),
        in_specs=[
            pl.BlockSpec(memory_space=pltpu.SMEM),
            pl.BlockSpec((1, c, dq), lambda bi: (bi, 0, 0)),
            pl.BlockSpec((1, c, dk), lambda bi: (bi, 0, 0)),
            pl.BlockSpec((1, c, dk), lambda bi: (bi, 0, 0)),
        ],
        out_specs=pl.BlockSpec((1, c, dq), lambda bi: (bi, 0, 0)),
        out_shape=jax.ShapeDtypeStruct((b, c, dq), BF16),
        compiler_params=_params(est, 1),
        name="ctx_gqa_attn",
    )(sink, qc, kc, vc)


def _diff_attn_kernel(lq1_ref, lk1_ref, lq2_ref, lk2_ref, g_ref, q_ref, k_ref, v_ref, kc_ref, vc_ref,
                      o_ref, k_all, v_all_t, *s_bufs, lam_init, tq):
    seq = k_ref.shape[1]
    n_tiles = seq // tq
    lam = (jnp.exp(jnp.sum(lq1_ref[...] * lk1_ref[...], axis=-1, keepdims=True))
           - jnp.exp(jnp.sum(lq2_ref[...] * lk2_ref[...], axis=-1, keepdims=True)) + lam_init)
    lane = lax.broadcasted_iota(jnp.int32, (tq, 2 * HEAD_DIM), 1)
    contract_last = (((1,), (1,)), ((), ()))
    out_gain = g_ref[...] * (1.0 - lam_init)

    hd = 2 * HEAD_DIM
    keys = k_all.shape[0]
    k_all[:seq, :] = k_ref[0]
    k_all[seq:, :] = kc_ref[0]
    v_all_t[:hd, :seq] = v_ref[0].astype(F32).T.astype(BF16)
    v_all_t[:hd, seq:] = vc_ref[0].astype(F32).T.astype(BF16)
    pad_row = lax.broadcasted_iota(jnp.int32, (v_all_t.shape[0] - hd, keys), 0)
    v_all_t[hd:, :] = jnp.where(pad_row == 0, 1.0, 0.0).astype(BF16)

    def scores(i, s_ref):
        q = q_ref[0, i * tq:(i + 1) * tq, :]
        zero = jnp.zeros_like(q)
        qq = jnp.concatenate([jnp.where(lane < HEAD_DIM, q, zero), jnp.where(lane >= HEAD_DIM, q, zero)], axis=0)
        s_ref[...] = lax.dot_general(k_all[...], qq, contract_last, preferred_element_type=F32)

    def finish(i, s_ref):
        s = s_ref[...]
        p = jnp.exp2(s - jnp.max(s, axis=0, keepdims=True)).astype(BF16)
        acc = jnp.dot(v_all_t[...], p, preferred_element_type=F32)
        n0, n1 = acc[hd:hd + 1, :tq], acc[hd:hd + 1, tq:]
        o_t = acc[:hd, :tq] / n0 - acc[:hd, tq:] * (lam / n1)
        o_t = o_t * lax.rsqrt(jnp.mean(o_t * o_t, axis=0, keepdims=True) + SUBLN_EPS)
        o_ref[0, i * tq:(i + 1) * tq, :] = (o_t * out_gain).T.astype(BF16)

    scores(0, s_bufs[0])
    for i in range(n_tiles):
        if i + 1 < n_tiles:
            scores(i + 1, s_bufs[(i + 1) % len(s_bufs)])
        finish(i, s_bufs[i % len(s_bufs)])


def _diff_attention(q, k, v, kc, vc, lq1, lk1, lq2, lk2, subln_g, lam_init, tq):
    b, l, dm = q.shape
    c = kc.shape[1]
    hd = 2 * HEAD_DIM
    vec = lambda n: pl.BlockSpec((1, n), lambda bi, h: (0, 0))
    per_head = lambda rows: pl.BlockSpec((1, rows, hd), lambda bi, h: (bi, 0, h))
    score_buf = pltpu.VMEM((l + c, 2 * tq), F32)
    est = 2 * (3 * l + 2 * c) * hd * 2 * 2 + 6 * 2 * tq * (l + c) * 4
    return pl.pallas_call(
        functools.partial(_diff_attn_kernel, lam_init=lam_init, tq=tq),
        grid=(b, dm // hd),
        in_specs=[
            vec(HEAD_DIM), vec(HEAD_DIM), vec(HEAD_DIM), vec(HEAD_DIM),
            pl.BlockSpec((hd, 1), lambda bi, h: (0, 0)),
            per_head(l), per_head(l), per_head(l), per_head(c), per_head(c),
        ],
        out_specs=per_head(l),
        out_shape=jax.ShapeDtypeStruct((b, l, dm), BF16),
        scratch_shapes=[pltpu.VMEM((l + c, hd), BF16), pltpu.VMEM((hd + BF16_SUBLANES, l + c), BF16),
                        score_buf, score_buf],
        compiler_params=_params(est, 2),
        name="diff_attn",
    )(lq1, lk1, lq2, lk2, subln_g, q, k, v, kc, vc)


def _post_kernel(a_ref, x_ref, mod_ref, wo_ref, w1_ref, w2_ref, ln_ref, o_ref, *, ff_chunk):
    d = x_ref.shape[-1]
    gate1 = mod_ref[0, :, 2 * d:3 * d]
    shift2 = mod_ref[0, :, 3 * d:4 * d]
    scale2 = mod_ref[0, :, 4 * d:5 * d]
    gate2 = mod_ref[0, :, 5 * d:6 * d]
    ax = jnp.dot(a_ref[0], wo_ref[...], preferred_element_type=F32)
    x1 = _layer_norm(DEEPNORM_ALPHA * x_ref[0] + gate1 * ax, ln_ref[0:1, :], ln_ref[1:2, :])
    h = (x1 * (1.0 + scale2) + shift2).astype(BF16)
    fx = jnp.zeros_like(x1)
    for j in range(w1_ref.shape[1] // ff_chunk):
        u = jnp.dot(h, w1_ref[:, j * ff_chunk:(j + 1) * ff_chunk], preferred_element_type=F32)
        u = jnp.maximum(u, 0.0)
        fx = fx + jnp.dot((u * u).astype(BF16), w2_ref[j * ff_chunk:(j + 1) * ff_chunk, :],
                          preferred_element_type=F32)
    o_ref[0] = _layer_norm(DEEPNORM_ALPHA * x1 + gate2 * fx, ln_ref[2:3, :], ln_ref[3:4, :])


def _post(attn, x, mod, wo, w1, w2, ln, tm):
    nb, rows, d = x.shape
    dff = w1.shape[1]
    ff_chunk = 1024
    mod_map = (lambda b, i: (b, 0, 0)) if mod.shape[0] == nb and nb > 1 else (lambda b, i: (0, 0, 0))
    const = lambda shape: pl.BlockSpec(shape, lambda b, i: (0, 0), pipeline_mode=pl.Buffered(1))
    est = ((d * d + 2 * d * dff) * 2 + 2 * tm * d * (2 + 4 + 4)
           + tm * d * 4 * 4 + tm * ff_chunk * 6)
    return pl.pallas_call(
        functools.partial(_post_kernel, ff_chunk=ff_chunk),
        grid=(nb, rows // tm),
        in_specs=[
            pl.BlockSpec((1, tm, d), lambda b, i: (b, i, 0)),
            pl.BlockSpec((1, tm, d), lambda b, i: (b, i, 0)),
            pl.BlockSpec((1, 1, mod.shape[-1]), mod_map),
            const(wo.shape), const(w1.shape), const(w2.shape),
            pl.BlockSpec(ln.shape, lambda b, i: (0, 0)),
        ],
        out_specs=pl.BlockSpec((1, tm, d), lambda b, i: (b, i, 0)),
        out_shape=jax.ShapeDtypeStruct((nb, rows, d), F32),
        compiler_params=_params(est, 2),
        name="oproj_ln_mlp_ln",
    )(attn, x, mod, wo, w1, w2, ln)


def _rope_tables(seq):
    rows = seq // GRID_W
    row = jnp.repeat(jnp.arange(rows, dtype=F32), GRID_W)
    col = jnp.tile(jnp.arange(GRID_W, dtype=F32), rows)
    n_freq = HEAD_DIM // 4
    inv = ROPE_BASE ** (-jnp.arange(n_freq, dtype=F32) / n_freq)
    ang = jnp.concatenate([row[:, None] * inv, col[:, None] * inv], axis=-1)
    cos, sin = jnp.cos(ang), jnp.sin(ang)
    reps = V7X_LANES // HEAD_DIM
    cos_t = jnp.tile(jnp.concatenate([cos, cos], axis=-1), (1, reps))
    sin_t = jnp.tile(jnp.concatenate([-sin, sin], axis=-1), (1, reps))
    return cos_t, sin_t


def kernel(x, c, ctx, c_ctx, w_ada, b_ada, ln1_g, ln1_b, ln2_g, ln2_b, a_wq, a_wk, a_wv, a_wo, a_sink,
           b_wq, b_wk, b_wv, b_wo, b_lq1, b_lk1, b_lq2, b_lk2, b_subln_g, mlp_w1, mlp_w2):
    batch, seq, d = x.shape
    ctx_len = ctx.shape[1]
    tables = _rope_tables(seq)
    bf = lambda w: w.astype(BF16)

    cond = jnp.concatenate([c, c_ctx[None, :], jnp.zeros((MOD_ROWS - batch - 1, d), F32)], axis=0)
    mod = _ada(cond, w_ada, b_ada[:, None, :])

    for i in range(DEPTH):
        need_ctx = i < DEPTH - 1
        mod_x = mod[i, :batch][:, None, :]
        mod_c = mod[i, batch:batch + 1][:, None, :]
        ln = jnp.stack([ln1_g[i], ln1_b[i], ln2_g[i], ln2_b[i]], axis=0)
        j = i // N_MIXERS
        if i % N_MIXERS == 0:
            wq = a_wq[j].reshape(d, A_KV_HEADS, A_GROUP, HEAD_DIM).swapaxes(1, 2).reshape(d, d)
            wo = a_wo[j].reshape(A_KV_HEADS, A_GROUP, HEAD_DIM, d).swapaxes(0, 1).reshape(d, d)
            wq, wk, wv, wo = bf(wq), bf(a_wk[j]), bf(a_wv[j]), bf(wo)
            q, k, v = _project(x, mod_x, [wq, wk, wv], tables, [QK_SCALE * LOG2_E, 1.0, None], tm=512)
            if need_ctx:
                qc, kc, vc = _project(ctx, mod_c, [wq, wk, wv], None, [QK_SCALE * LOG2_E, None, None],
                                      tm=ctx_len)
            else:
                kc, vc = _project(ctx, mod_c, [wk, wv], None, [None, None], tm=ctx_len)
            attn_x = _window_attention(q, k, v, kc, vc, a_sink[j])
            if need_ctx:
                attn_c = _ctx_attention(qc, kc, vc, a_sink[j])
        else:
            lam_init = 0.8 - 0.6 * math.exp(-0.3 * i)
            wq, wk, wv, wo = bf(b_wq[j]), bf(b_wk[j]), bf(b_wv[j]), bf(b_wo[j])
            q, k, v = _project(x, mod_x, [wq, wk, wv], tables, [QK_SCALE * LOG2_E, 1.0, None], tm=512)
            if need_ctx:
                raise NotImplementedError("context queries through differential attention")
            kc, vc = _project(ctx, mod_c, [wk, wv], None, [None, None], tm=ctx_len)
            attn_x = _diff_attention(q, k, v, kc, vc, b_lq1[j][None], b_lk1[j][None], b_lq2[j][None],
                                     b_lk2[j][None], b_subln_g[j][:, None], lam_init, tq=512)
        w1, w2 = bf(mlp_w1[i]), bf(mlp_w2[i])
        x = _post(attn_x, x, mod_x, wo, w1, w2, ln, tm=512)
        if need_ctx:
            ctx = _post(attn_c, ctx, mod_c, wo, w1, w2, ln, tm=ctx_len)
    return x
```

```python
import functools
import math

import jax
import jax.numpy as jnp
from jax import lax
from jax.experimental import pallas as pl
from jax.experimental.pallas import tpu as pltpu

D_MODEL = 1024
DEPTH = 2
GRID_W = 64
N_MIXERS = 2
BLOCK = 128
HEAD_DIM = 64
A_HEADS = D_MODEL // HEAD_DIM
A_KV_HEADS = max(1, A_HEADS // 8)
A_GROUP = A_HEADS // A_KV_HEADS
B_HEADS = D_MODEL // (2 * HEAD_DIM)
D_FF = 4 * D_MODEL
N_MOD = 6
ROPE_BASE = 10000.0
LN_EPS = 1e-5
SUBLN_EPS = 1e-5
NEG_INF = -1e30
DEEPNORM_ALPHA = (2 * DEPTH) ** 0.25
QK_SCALE = HEAD_DIM ** -0.5
LOG2_E = math.log2(math.e)

V7X_LANES = 128
BF16_SUBLANES = 16
V7X_VMEM_BYTES = 64 * 1024 * 1024
MOD_ROWS = 24

BF16 = jnp.bfloat16
F32 = jnp.float32


def _vmem_limit(estimate_bytes):
    return int(min(max(2 * estimate_bytes, 16 * 1024 * 1024), V7X_VMEM_BYTES - 8 * 1024 * 1024))


def _params(estimate_bytes, n_grid):
    return pltpu.CompilerParams(
        dimension_semantics=("arbitrary",) * n_grid,
        vmem_limit_bytes=_vmem_limit(estimate_bytes),
    )


def _layer_norm(y, g, b):
    mu = jnp.mean(y, axis=-1, keepdims=True)
    yc = y - mu
    var = jnp.mean(yc * yc, axis=-1, keepdims=True)
    return yc * lax.rsqrt(var + LN_EPS) * g + b


def _ada_kernel(c_ref, w_ref, b_ref, o_ref):
    c = c_ref[...]
    h = (c * jax.nn.sigmoid(c)).astype(BF16)
    acc = jnp.dot(h, w_ref[0].astype(BF16), preferred_element_type=F32)
    o_ref[0] = acc + b_ref[0]


def _ada(cond, w_ada, b_ada):
    depth, d, n = w_ada.shape
    tn = 1024
    est = 2 * d * tn * 4 + d * tn * 2 + MOD_ROWS * (d + 2 * tn) * 4
    return pl.pallas_call(
        _ada_kernel,
        grid=(depth, n // tn),
        in_specs=[
            pl.BlockSpec((MOD_ROWS, d), lambda i, j: (0, 0)),
            pl.BlockSpec((1, d, tn), lambda i, j: (i, 0, j)),
            pl.BlockSpec((1, 1, tn), lambda i, j: (i, 0, j)),
        ],
        out_specs=pl.BlockSpec((1, MOD_ROWS, tn), lambda i, j: (i, 0, j)),
        out_shape=jax.ShapeDtypeStruct((depth, MOD_ROWS, n), F32),
        compiler_params=_params(est, 2),
        name="ada_mod",
    )(cond, w_ada, b_ada)


def _rope(y, cos, sin_signed, first_half):
    half = HEAD_DIM // 2
    outs = []
    for g in range(y.shape[1] // V7X_LANES):
        yg = y[:, g * V7X_LANES:(g + 1) * V7X_LANES]
        partner = jnp.where(first_half,
                            pltpu.roll(yg, V7X_LANES - half, 1),
                            pltpu.roll(yg, half, 1))
        outs.append(yg * cos + partner * sin_signed)
    return outs


def _proj_kernel(*refs, n_out, rope, scales):
    x_ref, mod_ref = refs[0], refs[1]
    w_refs = refs[2:2 + n_out]
    pos = 2 + n_out
    if rope:
        cos_ref, sin_ref = refs[pos], refs[pos + 1]
        pos += 2
    o_refs = refs[pos:pos + n_out]
    d = x_ref.shape[-1]
    shift = mod_ref[0, :, 0:d]
    scale = mod_ref[0, :, d:2 * d]
    h = (x_ref[0] * (1.0 + scale) + shift).astype(BF16)
    if rope:
        cos = cos_ref[...]
        sin_signed = sin_ref[...]
        lane = lax.broadcasted_iota(jnp.int32, cos.shape, 1)
        first_half = (lane % HEAD_DIM) < (HEAD_DIM // 2)
    for j in range(n_out):
        y = jnp.dot(h, w_refs[j][...], preferred_element_type=F32)
        roped = rope and scales[j] is not None
        if roped:
            parts = _rope(y, cos, sin_signed, first_half)
            for g, part in enumerate(parts):
                if scales[j] != 1.0:
                    part = part * scales[j]
                o_refs[j][0, :, g * V7X_LANES:(g + 1) * V7X_LANES] = part.astype(BF16)
        else:
            if scales[j] is not None and scales[j] != 1.0:
                y = y * scales[j]
            o_refs[j][0] = y.astype(BF16)


def _project(x, mod, weights, rope_tables, scales, tm):
    nb, rows, d = x.shape
    n_out = len(weights)
    rope = rope_tables is not None
    mod_map = (lambda b, i: (b, 0, 0)) if mod.shape[0] == nb and nb > 1 else (lambda b, i: (0, 0, 0))
    in_specs = [
        pl.BlockSpec((1, tm, d), lambda b, i: (b, i, 0)),
        pl.BlockSpec((1, 1, mod.shape[-1]), mod_map),
    ]
    for w in weights:
        in_specs.append(pl.BlockSpec(w.shape, lambda b, i: (0, 0)))
    args = [x, mod] + list(weights)
    if rope:
        in_specs += [pl.BlockSpec((tm, V7X_LANES), lambda b, i: (i, 0))] * 2
        args += list(rope_tables)
    n_tot = sum(w.shape[1] for w in weights)
    est = 2 * tm * d * 4 + 2 * d * n_tot * 2 + 2 * tm * n_tot * 2 + tm * d * 2 + 2 * tm * n_tot * 4
    return pl.pallas_call(
        functools.partial(_proj_kernel, n_out=n_out, rope=rope, scales=tuple(scales)),
        grid=(nb, rows // tm),
        in_specs=in_specs,
        out_specs=[pl.BlockSpec((1, tm, w.shape[1]), lambda b, i: (b, i, 0)) for w in weights],
        out_shape=[jax.ShapeDtypeStruct((nb, rows, w.shape[1]), BF16) for w in weights],
        compiler_params=_params(est, 2),
        name="mod_proj_rope" if rope else "mod_proj",
    )(*args)


def _gqa_block(q, keys, vals, masks, sink_ref, o_ref, row0):
    tq = q.shape[0]
    width = A_KV_HEADS * HEAD_DIM
    contract_last = (((1,), (1,)), ((), ()))
    qs = jnp.concatenate([q[:, h * width:(h + 1) * width] for h in range(A_GROUP)], axis=0)
    q_lane = lax.broadcasted_iota(jnp.int32, qs.shape, 1)
    vals_t = vals.astype(F32).T.astype(BF16)
    masks = [(r0, jnp.concatenate([valid] * A_GROUP, axis=1)) for r0, valid in masks]
    outs = []
    for g in range(A_KV_HEADS):
        qg = jnp.where((q_lane >= g * HEAD_DIM) & (q_lane < (g + 1) * HEAD_DIM), qs, jnp.zeros_like(qs))
        s = lax.dot_general(keys, qg, contract_last, preferred_element_type=F32)
        pieces, row = [], 0
        for r0, valid in masks:
            if r0 > row:
                pieces.append(s[row:r0])
            pieces.append(jnp.where(valid, s[r0:r0 + valid.shape[0]], NEG_INF))
            row = r0 + valid.shape[0]
        if masks:
            s = jnp.concatenate(pieces + [s[row:]], axis=0)
        sink = jnp.concatenate(
            [jnp.full((1, tq), sink_ref[g * A_GROUP + h] * LOG2_E, F32) for h in range(A_GROUP)], axis=1)
        m = jnp.maximum(jnp.max(s, axis=0, keepdims=True), sink)
        p = jnp.exp2(s - m)
        inv = 1.0 / (jnp.sum(p, axis=0, keepdims=True) + jnp.exp2(sink - m))
        acc = jnp.dot(vals_t[g * HEAD_DIM:(g + 1) * HEAD_DIM, :], p.astype(BF16), preferred_element_type=F32)
        outs.append(acc * inv)
    out_t = jnp.concatenate(outs, axis=0)
    for h in range(A_GROUP):
        o_ref[0, row0:row0 + tq, h * width:(h + 1) * width] = out_t[:, h * tq:(h + 1) * tq].T.astype(BF16)


def _window_attn_kernel(sink_ref, q_ref, *refs, q_blocks):
    n_kv = q_blocks + 2
    k_refs, v_refs = refs[:n_kv], refs[n_kv:2 * n_kv]
    kc_ref, vc_ref, o_ref = refs[2 * n_kv:]
    first = pl.program_id(1) * q_blocks
    last = pl.num_programs(1) * q_blocks - 1
    key = lax.broadcasted_iota(jnp.int32, (BLOCK, BLOCK), 0)
    qry = lax.broadcasted_iota(jnp.int32, (BLOCK, BLOCK), 1)
    for t in range(q_blocks):
        masks = [
            (0, (key >= qry) & (first + t > 0)),
            (2 * BLOCK, (key <= qry) & (first + t < last)),
        ]
        keys = jnp.concatenate([k_refs[t][0], k_refs[t + 1][0], k_refs[t + 2][0], kc_ref[0]], axis=0)
        vals = jnp.concatenate([v_refs[t][0], v_refs[t + 1][0], v_refs[t + 2][0], vc_ref[0]], axis=0)
        _gqa_block(q_ref[0, t * BLOCK:(t + 1) * BLOCK, :], keys, vals, masks, sink_ref, o_ref, t * BLOCK)


def _window_attention(q, k, v, kc, vc, sink, q_blocks):
    b, l, dq = q.shape
    dk = k.shape[-1]
    c = kc.shape[1]
    nblk = l // BLOCK
    tq = q_blocks * BLOCK

    def kv_spec(j):
        return pl.BlockSpec((1, BLOCK, dk), lambda bi, n: (bi, jnp.clip(n * q_blocks - 1 + j, 0, nblk - 1), 0))

    kv_specs = [kv_spec(j) for j in range(q_blocks + 2)]
    ctx_spec = pl.BlockSpec((1, c, dk), lambda bi, n: (bi, 0, 0))
    est = 4 * tq * dq * 2 + 8 * (tq + 2 * BLOCK + c) * dk * 2 + 6 * q_blocks * A_GROUP * BLOCK * (3 * BLOCK + c) * 4
    return pl.pallas_call(
        functools.partial(_window_attn_kernel, q_blocks=q_blocks),
        grid=(b, nblk // q_blocks),
        in_specs=[pl.BlockSpec(memory_space=pltpu.SMEM), pl.BlockSpec((1, tq, dq), lambda bi, n: (bi, n, 0))]
        + kv_specs + kv_specs + [ctx_spec, ctx_spec],
        out_specs=pl.BlockSpec((1, tq, dq), lambda bi, n: (bi, n, 0)),
        out_shape=jax.ShapeDtypeStruct((b, l, dq), BF16),
        compiler_params=_params(est, 2),
        name="window_gqa_attn",
    )(sink, q, *([k] * (q_blocks + 2)), *([v] * (q_blocks + 2)), kc, vc)


def _ctx_attn_kernel(sink_ref, q_ref, kc_ref, vc_ref, o_ref):
    _gqa_block(q_ref[0], kc_ref[0], vc_ref[0], [], sink_ref, o_ref, 0)


def _ctx_attention(qc, kc, vc, sink):
    b, c, dq = qc.shape
    dk = kc.shape[-1]
    est = 4 * c * dq * 2 + 4 * c * dk * 2 + 6 * A_GROUP * c * c * 4
    return pl.pallas_call(
        _ctx_attn_kernel,
        grid=(b,),
        in_specs=[
            pl.BlockSpec(memory_space=pltpu.SMEM),
            pl.BlockSpec((1, c, dq), lambda bi: (bi, 0, 0)),
            pl.BlockSpec((1, c, dk), lambda bi: (bi, 0, 0)),
            pl.BlockSpec((1, c, dk), lambda bi: (bi, 0, 0)),
        ],
        out_specs=pl.BlockSpec((1, c, dq), lambda bi: (bi, 0, 0)),
        out_shape=jax.ShapeDtypeStruct((b, c, dq), BF16),
        compiler_params=_params(est, 1),
        name="ctx_gqa_attn",
    )(sink, qc, kc, vc)


def _diff_attn_kernel(lq1_ref, lk1_ref, lq2_ref, lk2_ref, g_ref, q_ref, k_ref, v_ref, kc_ref, vc_ref,
                      o_ref, k_all, v_all_t, *s_bufs, lam_init, tq):
    seq = k_ref.shape[1]
    n_tiles = seq // tq
    lam = (jnp.exp(jnp.sum(lq1_ref[...] * lk1_ref[...], axis=-1, keepdims=True))
           - jnp.exp(jnp.sum(lq2_ref[...] * lk2_ref[...], axis=-1, keepdims=True)) + lam_init)
    lane = lax.broadcasted_iota(jnp.int32, (tq, 2 * HEAD_DIM), 1)
    contract_last = (((1,), (1,)), ((), ()))
    out_gain = g_ref[...] * (1.0 - lam_init)

    hd = 2 * HEAD_DIM
    keys = k_all.shape[0]
    k_all[:seq, :] = k_ref[0]
    k_all[seq:, :] = kc_ref[0]
    v_all_t[:hd, :seq] = v_ref[0].astype(F32).T.astype(BF16)
    v_all_t[:hd, seq:] = vc_ref[0].astype(F32).T.astype(BF16)
    pad_row = lax.broadcasted_iota(jnp.int32, (v_all_t.shape[0] - hd, keys), 0)
    v_all_t[hd:, :] = jnp.where(pad_row == 0, 1.0, 0.0).astype(BF16)

    def scores(i, s_ref):
        q = q_ref[0, i * tq:(i + 1) * tq, :]
        zero = jnp.zeros_like(q)
        qq = jnp.concatenate([jnp.where(lane < HEAD_DIM, q, zero), jnp.where(lane >= HEAD_DIM, q, zero)], axis=0)
        s_ref[...] = lax.dot_general(k_all[...], qq, contract_last, preferred_element_type=F32)

    def finish(i, s_ref):
        s = s_ref[...]
        p = jnp.exp2(s - jnp.max(s, axis=0, keepdims=True)).astype(BF16)
        acc = jnp.dot(v_all_t[...], p, preferred_element_type=F32)
        n0, n1 = acc[hd:hd + 1, :tq], acc[hd:hd + 1, tq:]
        o_t = acc[:hd, :tq] / n0 - acc[:hd, tq:] * (lam / n1)
        o_t = o_t * lax.rsqrt(jnp.mean(o_t * o_t, axis=0, keepdims=True) + SUBLN_EPS)
        o_ref[0, i * tq:(i + 1) * tq, :] = (o_t * out_gain).T.astype(BF16)

    scores(0, s_bufs[0])
    for i in range(n_tiles):
        if i + 1 < n_tiles:
            scores(i + 1, s_bufs[(i + 1) % len(s_bufs)])
        finish(i, s_bufs[i % len(s_bufs)])


def _diff_attention(q, k, v, kc, vc, lq1, lk1, lq2, lk2, subln_g, lam_init, tq):
    b, l, dm = q.shape
    c = kc.shape[1]
    hd = 2 * HEAD_DIM
    vec = lambda n: pl.BlockSpec((1, n), lambda bi, h: (0, 0))
    per_head = lambda rows: pl.BlockSpec((1, rows, hd), lambda bi, h: (bi, 0, h))
    score_buf = pltpu.VMEM((l + c, 2 * tq), F32)
    est = 2 * (3 * l + 2 * c) * hd * 2 * 2 + 6 * 2 * tq * (l + c) * 4
    return pl.pallas_call(
        functools.partial(_diff_attn_kernel, lam_init=lam_init, tq=tq),
        grid=(b, dm // hd),
        in_specs=[
            vec(HEAD_DIM), vec(HEAD_DIM), vec(HEAD_DIM), vec(HEAD_DIM),
            pl.BlockSpec((hd, 1), lambda bi, h: (0, 0)),
            per_head(l), per_head(l), per_head(l), per_head(c), per_head(c),
        ],
        out_specs=per_head(l),
        out_shape=jax.ShapeDtypeStruct((b, l, dm), BF16),
        scratch_shapes=[pltpu.VMEM((l + c, hd), BF16), pltpu.VMEM((hd + BF16_SUBLANES, l + c), BF16),
                        score_buf, score_buf],
        compiler_params=_params(est, 2),
        name="diff_attn",
    )(lq1, lk1, lq2, lk2, subln_g, q, k, v, kc, vc)


def _post_kernel(a_ref, x_ref, mod_ref, wo_ref, w1_ref, w2_ref, ln_ref, o_ref, *, ff_chunk):
    d = x_ref.shape[-1]
    gate1 = mod_ref[0, :, 2 * d:3 * d]
    shift2 = mod_ref[0, :, 3 * d:4 * d]
    scale2 = mod_ref[0, :, 4 * d:5 * d]
    gate2 = mod_ref[0, :, 5 * d:6 * d]
    ax = jnp.dot(a_ref[0], wo_ref[...], preferred_element_type=F32)
    x1 = _layer_norm(DEEPNORM_ALPHA * x_ref[0] + gate1 * ax, ln_ref[0:1, :], ln_ref[1:2, :])
    h = (x1 * (1.0 + scale2) + shift2).astype(BF16)
    fx = jnp.zeros_like(x1)
    for j in range(w1_ref.shape[1] // ff_chunk):
        u = jnp.dot(h, w1_ref[:, j * ff_chunk:(j + 1) * ff_chunk], preferred_element_type=F32)
        u = jnp.maximum(u, 0.0)
        fx = fx + jnp.dot((u * u).astype(BF16), w2_ref[j * ff_chunk:(j + 1) * ff_chunk, :],
                          preferred_element_type=F32)
    o_ref[0] = _layer_norm(DEEPNORM_ALPHA * x1 + gate2 * fx, ln_ref[2:3, :], ln_ref[3:4, :])


def _post(attn, x, mod, wo, w1, w2, ln, tm):
    nb, rows, d = x.shape
    dff = w1.shape[1]
    ff_chunk = 1024
    mod_map = (lambda b, i: (b, 0, 0)) if mod.shape[0] == nb and nb > 1 else (lambda b, i: (0, 0, 0))
    const = lambda shape: pl.BlockSpec(shape, lambda b, i: (0, 0), pipeline_mode=pl.Buffered(1))
    est = ((d * d + 2 * d * dff) * 2 + 2 * tm * d * (2 + 4 + 4)
           + tm * d * 4 * 4 + tm * ff_chunk * 6)
    return pl.pallas_call(
        functools.partial(_post_kernel, ff_chunk=ff_chunk),
        grid=(nb, rows // tm),
        in_specs=[
            pl.BlockSpec((1, tm, d), lambda b, i: (b, i, 0)),
            pl.BlockSpec((1, tm, d), lambda b, i: (b, i, 0)),
            pl.BlockSpec((1, 1, mod.shape[-1]), mod_map),
            const(wo.shape), const(w1.shape), const(w2.shape),
            pl.BlockSpec(ln.shape, lambda b, i: (0, 0)),
        ],
        out_specs=pl.BlockSpec((1, tm, d), lambda b, i: (b, i, 0)),
        out_shape=jax.ShapeDtypeStruct((nb, rows, d), F32),
        compiler_params=_params(est, 2),
        name="oproj_ln_mlp_ln",
    )(attn, x, mod, wo, w1, w2, ln)


def _rope_tables(seq):
    rows = seq // GRID_W
    row = jnp.repeat(jnp.arange(rows, dtype=F32), GRID_W)
    col = jnp.tile(jnp.arange(GRID_W, dtype=F32), rows)
    n_freq = HEAD_DIM // 4
    inv = ROPE_BASE ** (-jnp.arange(n_freq, dtype=F32) / n_freq)
    ang = jnp.concatenate([row[:, None] * inv, col[:, None] * inv], axis=-1)
    cos, sin = jnp.cos(ang), jnp.sin(ang)
    reps = V7X_LANES // HEAD_DIM
    cos_t = jnp.tile(jnp.concatenate([cos, cos], axis=-1), (1, reps))
    sin_t = jnp.tile(jnp.concatenate([-sin, sin], axis=-1), (1, reps))
    return cos_t, sin_t


def kernel(x, c, ctx, c_ctx, w_ada, b_ada, ln1_g, ln1_b, ln2_g, ln2_b, a_wq, a_wk, a_wv, a_wo, a_sink,
           b_wq, b_wk, b_wv, b_wo, b_lq1, b_lk1, b_lq2, b_lk2, b_subln_g, mlp_w1, mlp_w2):
    batch, seq, d = x.shape
    ctx_len = ctx.shape[1]
    tables = _rope_tables(seq)
    bf = lambda w: w.astype(BF16)

    cond = jnp.concatenate([c, c_ctx[None, :], jnp.zeros((MOD_ROWS - batch - 1, d), F32)], axis=0)
    mod = _ada(cond, w_ada, b_ada[:, None, :])

    ctx_rows = 512

    def ctx_project(ctx, mod_c, weights, scales):
        outs = _project(ctx.reshape(1, batch * ctx_len, d), mod_c, weights, None, scales, tm=ctx_rows)
        return [o.reshape(batch, ctx_len, o.shape[-1]) for o in outs]

    for i in range(DEPTH):
        need_ctx = i < DEPTH - 1
        mod_x = mod[i, :batch][:, None, :]
        mod_c = mod[i, batch:batch + 1][:, None, :]
        ln = jnp.stack([ln1_g[i], ln1_b[i], ln2_g[i], ln2_b[i]], axis=0)
        j = i // N_MIXERS
        if i % N_MIXERS == 0:
            wq = a_wq[j].reshape(d, A_KV_HEADS, A_GROUP, HEAD_DIM).swapaxes(1, 2).reshape(d, d)
            wo = a_wo[j].reshape(A_KV_HEADS, A_GROUP, HEAD_DIM, d).swapaxes(0, 1).reshape(d, d)
            wq, wk, wv, wo = bf(wq), bf(a_wk[j]), bf(a_wv[j]), bf(wo)
            q, k, v = _project(x, mod_x, [wq, wk, wv], tables, [QK_SCALE * LOG2_E, 1.0, None], tm=512)
            if need_ctx:
                qc, kc, vc = ctx_project(ctx, mod_c, [wq, wk, wv], [QK_SCALE * LOG2_E, None, None])
            else:
                kc, vc = ctx_project(ctx, mod_c, [wk, wv], [None, None])
            attn_x = _window_attention(q, k, v, kc, vc, a_sink[j], q_blocks=2)
            if need_ctx:
                attn_c = _ctx_attention(qc, kc, vc, a_sink[j])
        else:
            lam_init = 0.8 - 0.6 * math.exp(-0.3 * i)
            wq, wk, wv, wo = bf(b_wq[j]), bf(b_wk[j]), bf(b_wv[j]), bf(b_wo[j])
            q, k, v = _project(x, mod_x, [wq, wk, wv], tables, [QK_SCALE * LOG2_E, 1.0, None], tm=512)
            if need_ctx:
                raise NotImplementedError("context queries through differential attention")
            kc, vc = ctx_project(ctx, mod_c, [wk, wv], [None, None])
            attn_x = _diff_attention(q, k, v, kc, vc, b_lq1[j][None], b_lk1[j][None], b_lq2[j][None],
                                     b_lk2[j][None], b_subln_g[j][:, None], lam_init, tq=512)
        w1, w2 = bf(mlp_w1[i]), bf(mlp_w2[i])
        x = _post(attn_x, x, mod_x, wo, w1, w2, ln, tm=512)
        if need_ctx:
            flat = (1, batch * ctx_len, d)
            ctx = _post(attn_c.reshape(flat), ctx.reshape(flat), mod_c, wo, w1, w2, ln, tm=ctx_rows)
            ctx = ctx.reshape(batch, ctx_len, d)
    return x
```

```python
import functools
import math

import jax
import jax.numpy as jnp
from jax import lax
from jax.experimental import pallas as pl
from jax.experimental.pallas import tpu as pltpu

D_MODEL = 1024
DEPTH = 2
GRID_W = 64
N_MIXERS = 2
BLOCK = 128
HEAD_DIM = 64
A_HEADS = D_MODEL // HEAD_DIM
A_KV_HEADS = max(1, A_HEADS // 8)
A_GROUP = A_HEADS // A_KV_HEADS
B_HEADS = D_MODEL // (2 * HEAD_DIM)
D_FF = 4 * D_MODEL
N_MOD = 6
ROPE_BASE = 10000.0
LN_EPS = 1e-5
SUBLN_EPS = 1e-5
NEG_INF = -1e30
DEEPNORM_ALPHA = (2 * DEPTH) ** 0.25
QK_SCALE = HEAD_DIM ** -0.5
LOG2_E = math.log2(math.e)

V7X_LANES = 128
BF16_SUBLANES = 16
V7X_VMEM_BYTES = 64 * 1024 * 1024
MOD_ROWS = 24

BF16 = jnp.bfloat16
F32 = jnp.float32


def _vmem_limit(estimate_bytes):
    return int(min(max(2 * estimate_bytes, 16 * 1024 * 1024), V7X_VMEM_BYTES - 8 * 1024 * 1024))


def _params(estimate_bytes, n_grid):
    return pltpu.CompilerParams(
        dimension_semantics=("arbitrary",) * n_grid,
        vmem_limit_bytes=_vmem_limit(estimate_bytes),
    )


def _layer_norm(y, g, b):
    mu = jnp.mean(y, axis=-1, keepdims=True)
    yc = y - mu
    var = jnp.mean(yc * yc, axis=-1, keepdims=True)
    return yc * lax.rsqrt(var + LN_EPS) * g + b


def _ada_kernel(c_ref, w_ref, b_ref, o_ref):
    c = c_ref[...]
    h = (c * jax.nn.sigmoid(c)).astype(BF16)
    acc = jnp.dot(h, w_ref[0].astype(BF16), preferred_element_type=F32)
    o_ref[0] = acc + b_ref[0]


def _ada(cond, w_ada, b_ada):
    depth, d, n = w_ada.shape
    tn = 1024
    est = 2 * d * tn * 4 + d * tn * 2 + MOD_ROWS * (d + 2 * tn) * 4
    return pl.pallas_call(
        _ada_kernel,
        grid=(depth, n // tn),
        in_specs=[
            pl.BlockSpec((MOD_ROWS, d), lambda i, j: (0, 0)),
            pl.BlockSpec((1, d, tn), lambda i, j: (i, 0, j)),
            pl.BlockSpec((1, 1, tn), lambda i, j: (i, 0, j)),
        ],
        out_specs=pl.BlockSpec((1, MOD_ROWS, tn), lambda i, j: (i, 0, j)),
        out_shape=jax.ShapeDtypeStruct((depth, MOD_ROWS, n), F32),
        compiler_params=_params(est, 2),
        name="ada_mod",
    )(cond, w_ada, b_ada)


def _rope(y, cos, sin_signed, first_half):
    half = HEAD_DIM // 2
    outs = []
    for g in range(y.shape[1] // V7X_LANES):
        yg = y[:, g * V7X_LANES:(g + 1) * V7X_LANES]
        partner = jnp.where(first_half,
                            pltpu.roll(yg, V7X_LANES - half, 1),
                            pltpu.roll(yg, half, 1))
        outs.append(yg * cos + partner * sin_signed)
    return outs


def _proj_kernel(*refs, n_out, rope, scales):
    x_ref, mod_ref = refs[0], refs[1]
    w_refs = refs[2:2 + n_out]
    pos = 2 + n_out
    if rope:
        cos_ref, sin_ref = refs[pos], refs[pos + 1]
        pos += 2
    o_refs = refs[pos:pos + n_out]
    d = x_ref.shape[-1]
    shift = mod_ref[0, :, 0:d]
    scale = mod_ref[0, :, d:2 * d]
    h = (x_ref[0] * (1.0 + scale) + shift).astype(BF16)
    if rope:
        cos = cos_ref[...]
        sin_signed = sin_ref[...]
        lane = lax.broadcasted_iota(jnp.int32, cos.shape, 1)
        first_half = (lane % HEAD_DIM) < (HEAD_DIM // 2)
    for j in range(n_out):
        y = jnp.dot(h, w_refs[j][...], preferred_element_type=F32)
        roped = rope and scales[j] is not None
        if roped:
            parts = _rope(y, cos, sin_signed, first_half)
            for g, part in enumerate(parts):
                if scales[j] != 1.0:
                    part = part * scales[j]
                o_refs[j][0, :, g * V7X_LANES:(g + 1) * V7X_LANES] = part.astype(BF16)
        else:
            if scales[j] is not None and scales[j] != 1.0:
                y = y * scales[j]
            o_refs[j][0] = y.astype(BF16)


def _project(x, mod, weights, rope_tables, scales, tm):
    nb, rows, d = x.shape
    n_out = len(weights)
    rope = rope_tables is not None
    mod_map = (lambda b, i: (b, 0, 0)) if mod.shape[0] == nb and nb > 1 else (lambda b, i: (0, 0, 0))
    in_specs = [
        pl.BlockSpec((1, tm, d), lambda b, i: (b, i, 0)),
        pl.BlockSpec((1, 1, mod.shape[-1]), mod_map),
    ]
    for w in weights:
        in_specs.append(pl.BlockSpec(w.shape, lambda b, i: (0, 0)))
    args = [x, mod] + list(weights)
    if rope:
        in_specs += [pl.BlockSpec((tm, V7X_LANES), lambda b, i: (i, 0))] * 2
        args += list(rope_tables)
    n_tot = sum(w.shape[1] for w in weights)
    est = 2 * tm * d * 4 + 2 * d * n_tot * 2 + 2 * tm * n_tot * 2 + tm * d * 2 + 2 * tm * n_tot * 4
    return pl.pallas_call(
        functools.partial(_proj_kernel, n_out=n_out, rope=rope, scales=tuple(scales)),
        grid=(nb, rows // tm),
        in_specs=in_specs,
        out_specs=[pl.BlockSpec((1, tm, w.shape[1]), lambda b, i: (b, i, 0)) for w in weights],
        out_shape=[jax.ShapeDtypeStruct((nb, rows, w.shape[1]), BF16) for w in weights],
        compiler_params=_params(est, 2),
        name="mod_proj_rope" if rope else "mod_proj",
    )(*args)


def _gqa_block(q, keys, vals, masks, sink_ref, o_ref, row0):
    tq = q.shape[0]
    width = A_KV_HEADS * HEAD_DIM
    contract_last = (((1,), (1,)), ((), ()))
    qs = jnp.concatenate([q[:, h * width:(h + 1) * width] for h in range(A_GROUP)], axis=0)
    q_lane = lax.broadcasted_iota(jnp.int32, qs.shape, 1)
    vals_t = vals.astype(F32).T.astype(BF16)
    masks = [(r0, jnp.concatenate([valid] * A_GROUP, axis=1)) for r0, valid in masks]
    outs = []
    for g in range(A_KV_HEADS):
        qg = jnp.where((q_lane >= g * HEAD_DIM) & (q_lane < (g + 1) * HEAD_DIM), qs, jnp.zeros_like(qs))
        s = lax.dot_general(keys, qg, contract_last, preferred_element_type=F32)
        pieces, row = [], 0
        for r0, valid in masks:
            if r0 > row:
                pieces.append(s[row:r0])
            pieces.append(jnp.where(valid, s[r0:r0 + valid.shape[0]], NEG_INF))
            row = r0 + valid.shape[0]
        if masks:
            s = jnp.concatenate(pieces + [s[row:]], axis=0)
        sink = jnp.concatenate(
            [jnp.full((1, tq), sink_ref[g * A_GROUP + h] * LOG2_E, F32) for h in range(A_GROUP)], axis=1)
        m = jnp.maximum(jnp.max(s, axis=0, keepdims=True), sink)
        p = jnp.exp2(s - m)
        inv = 1.0 / (jnp.sum(p, axis=0, keepdims=True) + jnp.exp2(sink - m))
        acc = jnp.dot(vals_t[g * HEAD_DIM:(g + 1) * HEAD_DIM, :], p.astype(BF16), preferred_element_type=F32)
        outs.append(acc * inv)
    out_t = jnp.concatenate(outs, axis=0)
    for h in range(A_GROUP):
        o_ref[0, row0:row0 + tq, h * width:(h + 1) * width] = out_t[:, h * tq:(h + 1) * tq].T.astype(BF16)


def _window_attn_kernel(sink_ref, q_ref, *refs, q_blocks):
    n_kv = q_blocks + 2
    k_refs, v_refs = refs[:n_kv], refs[n_kv:2 * n_kv]
    kc_ref, vc_ref, o_ref = refs[2 * n_kv:]
    first = pl.program_id(1) * q_blocks
    last = pl.num_programs(1) * q_blocks - 1
    key = lax.broadcasted_iota(jnp.int32, (BLOCK, BLOCK), 0)
    qry = lax.broadcasted_iota(jnp.int32, (BLOCK, BLOCK), 1)
    for t in range(q_blocks):
        masks = [
            (0, (key >= qry) & (first + t > 0)),
            (2 * BLOCK, (key <= qry) & (first + t < last)),
        ]
        keys = jnp.concatenate([k_refs[t][0], k_refs[t + 1][0], k_refs[t + 2][0], kc_ref[0]], axis=0)
        vals = jnp.concatenate([v_refs[t][0], v_refs[t + 1][0], v_refs[t + 2][0], vc_ref[0]], axis=0)
        _gqa_block(q_ref[0, t * BLOCK:(t + 1) * BLOCK, :], keys, vals, masks, sink_ref, o_ref, t * BLOCK)


def _window_attention(q, k, v, kc, vc, sink, q_blocks):
    b, l, dq = q.shape
    dk = k.shape[-1]
    c = kc.shape[1]
    nblk = l // BLOCK
    tq = q_blocks * BLOCK

    def kv_spec(j):
        return pl.BlockSpec((1, BLOCK, dk), lambda bi, n: (bi, jnp.clip(n * q_blocks - 1 + j, 0, nblk - 1), 0))

    kv_specs = [kv_spec(j) for j in range(q_blocks + 2)]
    ctx_spec = pl.BlockSpec((1, c, dk), lambda bi, n: (bi, 0, 0))
    est = 4 * tq * dq * 2 + 8 * (tq + 2 * BLOCK + c) * dk * 2 + 6 * q_blocks * A_GROUP * BLOCK * (3 * BLOCK + c) * 4
    return pl.pallas_call(
        functools.partial(_window_attn_kernel, q_blocks=q_blocks),
        grid=(b, nblk // q_blocks),
        in_specs=[pl.BlockSpec(memory_space=pltpu.SMEM), pl.BlockSpec((1, tq, dq), lambda bi, n: (bi, n, 0))]
        + kv_specs + kv_specs + [ctx_spec, ctx_spec],
        out_specs=pl.BlockSpec((1, tq, dq), lambda bi, n: (bi, n, 0)),
        out_shape=jax.ShapeDtypeStruct((b, l, dq), BF16),
        compiler_params=_params(est, 2),
        name="window_gqa_attn",
    )(sink, q, *([k] * (q_blocks + 2)), *([v] * (q_blocks + 2)), kc, vc)


def _ctx_attn_kernel(sink_ref, q_ref, kc_ref, vc_ref, o_ref):
    _gqa_block(q_ref[0], kc_ref[0], vc_ref[0], [], sink_ref, o_ref, 0)


def _ctx_attention(qc, kc, vc, sink):
    b, c, dq = qc.shape
    dk = kc.shape[-1]
    est = 4 * c * dq * 2 + 4 * c * dk * 2 + 6 * A_GROUP * c * c * 4
    return pl.pallas_call(
        _ctx_attn_kernel,
        grid=(b,),
        in_specs=[
            pl.BlockSpec(memory_space=pltpu.SMEM),
            pl.BlockSpec((1, c, dq), lambda bi: (bi, 0, 0)),
            pl.BlockSpec((1, c, dk), lambda bi: (bi, 0, 0)),
            pl.BlockSpec((1, c, dk), lambda bi: (bi, 0, 0)),
        ],
        out_specs=pl.BlockSpec((1, c, dq), lambda bi: (bi, 0, 0)),
        out_shape=jax.ShapeDtypeStruct((b, c, dq), BF16),
        compiler_params=_params(est, 1),
        name="ctx_gqa_attn",
    )(sink, qc, kc, vc)


def _diff_attn_kernel(lq1_ref, lk1_ref, lq2_ref, lk2_ref, g_ref, q_ref, k_ref, v_ref, kc_ref, vc_ref,
                      o_ref, k_all, v_all_t, *s_bufs, lam_init, tq, heads):
    seq = k_ref.shape[1]
    n_tiles = seq // tq
    hd = 2 * HEAD_DIM
    keys = k_all.shape[1]
    lam = (jnp.exp(jnp.sum(lq1_ref[...] * lk1_ref[...], axis=-1, keepdims=True))
           - jnp.exp(jnp.sum(lq2_ref[...] * lk2_ref[...], axis=-1, keepdims=True)) + lam_init)
    lane = lax.broadcasted_iota(jnp.int32, (tq, hd), 1)
    contract_last = (((1,), (1,)), ((), ()))
    out_gain = g_ref[...] * (1.0 - lam_init)

    pad_row = lax.broadcasted_iota(jnp.int32, (v_all_t.shape[1] - hd, keys), 0)
    for h in range(heads):
        cols = slice(h * hd, (h + 1) * hd)
        k_all[h, :seq, :] = k_ref[0, :, cols]
        k_all[h, seq:, :] = kc_ref[0, :, cols]
        v_all_t[h, :hd, :seq] = v_ref[0, :, cols].astype(F32).T.astype(BF16)
        v_all_t[h, :hd, seq:] = vc_ref[0, :, cols].astype(F32).T.astype(BF16)
        v_all_t[h, hd:, :] = jnp.where(pad_row == 0, 1.0, 0.0).astype(BF16)

    def scores(h, i, s_ref):
        q = q_ref[0, i * tq:(i + 1) * tq, h * hd:(h + 1) * hd]
        zero = jnp.zeros_like(q)
        qq = jnp.concatenate([jnp.where(lane < HEAD_DIM, q, zero), jnp.where(lane >= HEAD_DIM, q, zero)], axis=0)
        s_ref[...] = lax.dot_general(k_all[h], qq, contract_last, preferred_element_type=F32)

    def finish(h, i, s_ref):
        s = s_ref[...]
        p = jnp.exp2(s - jnp.max(s, axis=0, keepdims=True)).astype(BF16)
        acc = jnp.dot(v_all_t[h], p, preferred_element_type=F32)
        n0, n1 = acc[hd:hd + 1, :tq], acc[hd:hd + 1, tq:]
        o_t = acc[:hd, :tq] / n0 - acc[:hd, tq:] * (lam / n1)
        o_t = o_t * lax.rsqrt(jnp.mean(o_t * o_t, axis=0, keepdims=True) + SUBLN_EPS)
        o_ref[0, i * tq:(i + 1) * tq, h * hd:(h + 1) * hd] = (o_t * out_gain).T.astype(BF16)

    work = [(h, i) for i in range(n_tiles) for h in range(heads)]
    scores(*work[0], s_bufs[0])
    for n, (h, i) in enumerate(work):
        if n + 1 < len(work):
            scores(*work[n + 1], s_bufs[(n + 1) % len(s_bufs)])
        finish(h, i, s_bufs[n % len(s_bufs)])


def _diff_attention(q, k, v, kc, vc, lq1, lk1, lq2, lk2, subln_g, lam_init, tq, heads):
    b, l, dm = q.shape
    c = kc.shape[1]
    hd = 2 * HEAD_DIM
    vec = lambda n: pl.BlockSpec((1, n), lambda bi, h: (0, 0))
    per_group = lambda rows: pl.BlockSpec((1, rows, heads * hd), lambda bi, h: (bi, 0, h))
    score_buf = pltpu.VMEM((l + c, 2 * tq), F32)
    est = 2 * heads * (3 * l + 2 * c) * hd * 2 * 2 + 6 * 2 * tq * (l + c) * 4
    return pl.pallas_call(
        functools.partial(_diff_attn_kernel, lam_init=lam_init, tq=tq, heads=heads),
        grid=(b, dm // (heads * hd)),
        in_specs=[
            vec(HEAD_DIM), vec(HEAD_DIM), vec(HEAD_DIM), vec(HEAD_DIM),
            pl.BlockSpec((hd, 1), lambda bi, h: (0, 0)),
            per_group(l), per_group(l), per_group(l), per_group(c), per_group(c),
        ],
        out_specs=per_group(l),
        out_shape=jax.ShapeDtypeStruct((b, l, dm), BF16),
        scratch_shapes=[pltpu.VMEM((heads, l + c, hd), BF16),
                        pltpu.VMEM((heads, hd + BF16_SUBLANES, l + c), BF16), score_buf, score_buf],
        compiler_params=_params(est, 2),
        name="diff_attn",
    )(lq1, lk1, lq2, lk2, subln_g, q, k, v, kc, vc)


def _post_kernel(a_ref, x_ref, mod_ref, wo_ref, w1_ref, w2_ref, ln_ref, o_ref, *, ff_chunk):
    d = x_ref.shape[-1]
    gate1 = mod_ref[0, :, 2 * d:3 * d]
    shift2 = mod_ref[0, :, 3 * d:4 * d]
    scale2 = mod_ref[0, :, 4 * d:5 * d]
    gate2 = mod_ref[0, :, 5 * d:6 * d]
    ax = jnp.dot(a_ref[0], wo_ref[...], preferred_element_type=F32)
    x1 = _layer_norm(DEEPNORM_ALPHA * x_ref[0] + gate1 * ax, ln_ref[0:1, :], ln_ref[1:2, :])
    h = (x1 * (1.0 + scale2) + shift2).astype(BF16)
    fx = jnp.zeros_like(x1)
    for j in range(w1_ref.shape[1] // ff_chunk):
        u = jnp.dot(h, w1_ref[:, j * ff_chunk:(j + 1) * ff_chunk], preferred_element_type=F32)
        u = jnp.maximum(u, 0.0)
        fx = fx + jnp.dot((u * u).astype(BF16), w2_ref[j * ff_chunk:(j + 1) * ff_chunk, :],
                          preferred_element_type=F32)
    o_ref[0] = _layer_norm(DEEPNORM_ALPHA * x1 + gate2 * fx, ln_ref[2:3, :], ln_ref[3:4, :])


def _post(attn, x, mod, wo, w1, w2, ln, tm):
    nb, rows, d = x.shape
    dff = w1.shape[1]
    ff_chunk = 1024
    mod_map = (lambda b, i: (b, 0, 0)) if mod.shape[0] == nb and nb > 1 else (lambda b, i: (0, 0, 0))
    const = lambda shape: pl.BlockSpec(shape, lambda b, i: (0, 0), pipeline_mode=pl.Buffered(1))
    est = ((d * d + 2 * d * dff) * 2 + 2 * tm * d * (2 + 4 + 4)
           + tm * d * 4 * 4 + tm * ff_chunk * 6)
    return pl.pallas_call(
        functools.partial(_post_kernel, ff_chunk=ff_chunk),
        grid=(nb, rows // tm),
        in_specs=[
            pl.BlockSpec((1, tm, d), lambda b, i: (b, i, 0)),
            pl.BlockSpec((1, tm, d), lambda b, i: (b, i, 0)),
            pl.BlockSpec((1, 1, mod.shape[-1]), mod_map),
            const(wo.shape), const(w1.shape), const(w2.shape),
            pl.BlockSpec(ln.shape, lambda b, i: (0, 0)),
        ],
        out_specs=pl.BlockSpec((1, tm, d), lambda b, i: (b, i, 0)),
        out_shape=jax.ShapeDtypeStruct((nb, rows, d), F32),
        compiler_params=_params(est, 2),
        name="oproj_ln_mlp_ln",
    )(attn, x, mod, wo, w1, w2, ln)


def _rope_tables(seq):
    rows = seq // GRID_W
    row = jnp.repeat(jnp.arange(rows, dtype=F32), GRID_W)
    col = jnp.tile(jnp.arange(GRID_W, dtype=F32), rows)
    n_freq = HEAD_DIM // 4
    inv = ROPE_BASE ** (-jnp.arange(n_freq, dtype=F32) / n_freq)
    ang = jnp.concatenate([row[:, None] * inv, col[:, None] * inv], axis=-1)
    cos, sin = jnp.cos(ang), jnp.sin(ang)
    reps = V7X_LANES // HEAD_DIM
    cos_t = jnp.tile(jnp.concatenate([cos, cos], axis=-1), (1, reps))
    sin_t = jnp.tile(jnp.concatenate([-sin, sin], axis=-1), (1, reps))
    return cos_t, sin_t


def kernel(x, c, ctx, c_ctx, w_ada, b_ada, ln1_g, ln1_b, ln2_g, ln2_b, a_wq, a_wk, a_wv, a_wo, a_sink,
           b_wq, b_wk, b_wv, b_wo, b_lq1, b_lk1, b_lq2, b_lk2, b_subln_g, mlp_w1, mlp_w2):
    batch, seq, d = x.shape
    ctx_len = ctx.shape[1]
    tables = _rope_tables(seq)
    bf = lambda w: w.astype(BF16)

    cond = jnp.concatenate([c, c_ctx[None, :], jnp.zeros((MOD_ROWS - batch - 1, d), F32)], axis=0)
    mod = _ada(cond, w_ada, b_ada[:, None, :])

    ctx_rows = 512

    def ctx_project(ctx, mod_c, weights, scales):
        outs = _project(ctx.reshape(1, batch * ctx_len, d), mod_c, weights, None, scales, tm=ctx_rows)
        return [o.reshape(batch, ctx_len, o.shape[-1]) for o in outs]

    for i in range(DEPTH):
        need_ctx = i < DEPTH - 1
        mod_x = mod[i, :batch][:, None, :]
        mod_c = mod[i, batch:batch + 1][:, None, :]
        ln = jnp.stack([ln1_g[i], ln1_b[i], ln2_g[i], ln2_b[i]], axis=0)
        j = i // N_MIXERS
        if i % N_MIXERS == 0:
            wq = a_wq[j].reshape(d, A_KV_HEADS, A_GROUP, HEAD_DIM).swapaxes(1, 2).reshape(d, d)
            wo = a_wo[j].reshape(A_KV_HEADS, A_GROUP, HEAD_DIM, d).swapaxes(0, 1).reshape(d, d)
            wq, wk, wv, wo = bf(wq), bf(a_wk[j]), bf(a_wv[j]), bf(wo)
            q, k, v = _project(x, mod_x, [wq, wk, wv], tables, [QK_SCALE * LOG2_E, 1.0, None], tm=512)
            if need_ctx:
                qc, kc, vc = ctx_project(ctx, mod_c, [wq, wk, wv], [QK_SCALE * LOG2_E, None, None])
            else:
                kc, vc = ctx_project(ctx, mod_c, [wk, wv], [None, None])
            attn_x = _window_attention(q, k, v, kc, vc, a_sink[j], q_blocks=4)
            if need_ctx:
                attn_c = _ctx_attention(qc, kc, vc, a_sink[j])
        else:
            lam_init = 0.8 - 0.6 * math.exp(-0.3 * i)
            wq, wk, wv, wo = bf(b_wq[j]), bf(b_wk[j]), bf(b_wv[j]), bf(b_wo[j])
            q, k, v = _project(x, mod_x, [wq, wk, wv], tables, [QK_SCALE * LOG2_E, 1.0, None], tm=512)
            if need_ctx:
                raise NotImplementedError("context queries through differential attention")
            kc, vc = ctx_project(ctx, mod_c, [wk, wv], [None, None])
            attn_x = _diff_attention(q, k, v, kc, vc, b_lq1[j][None], b_lk1[j][None], b_lq2[j][None],
                                     b_lk2[j][None], b_subln_g[j][:, None], lam_init, tq=512, heads=2)
        w1, w2 = bf(mlp_w1[i]), bf(mlp_w2[i])
        x = _post(attn_x, x, mod_x, wo, w1, w2, ln, tm=512)
        if need_ctx:
            flat = (1, batch * ctx_len, d)
            ctx = _post(attn_c.reshape(flat), ctx.reshape(flat), mod_c, wo, w1, w2, ln, tm=ctx_rows)
            ctx = ctx.reshape(batch, ctx_len, d)
    return x
```

```python
import functools
import math

import jax
import jax.numpy as jnp
from jax import lax
from jax.experimental import pallas as pl
from jax.experimental.pallas import tpu as pltpu

D_MODEL = 1024
DEPTH = 2
GRID_W = 64
N_MIXERS = 2
BLOCK = 128
HEAD_DIM = 64
A_HEADS = D_MODEL // HEAD_DIM
A_KV_HEADS = max(1, A_HEADS // 8)
A_GROUP = A_HEADS // A_KV_HEADS
B_HEADS = D_MODEL // (2 * HEAD_DIM)
D_FF = 4 * D_MODEL
N_MOD = 6
ROPE_BASE = 10000.0
LN_EPS = 1e-5
SUBLN_EPS = 1e-5
NEG_INF = -1e30
DEEPNORM_ALPHA = (2 * DEPTH) ** 0.25
QK_SCALE = HEAD_DIM ** -0.5
LOG2_E = math.log2(math.e)

V7X_LANES = 128
BF16_SUBLANES = 16
V7X_VMEM_BYTES = 64 * 1024 * 1024
MOD_ROWS = 24

BF16 = jnp.bfloat16
F32 = jnp.float32


def _vmem_limit(estimate_bytes):
    return int(min(max(2 * estimate_bytes, 16 * 1024 * 1024), V7X_VMEM_BYTES - 8 * 1024 * 1024))


def _params(estimate_bytes, n_grid):
    return pltpu.CompilerParams(
        dimension_semantics=("arbitrary",) * n_grid,
        vmem_limit_bytes=_vmem_limit(estimate_bytes),
    )


def _layer_norm(y, g, b):
    mu = jnp.mean(y, axis=-1, keepdims=True)
    yc = y - mu
    var = jnp.mean(yc * yc, axis=-1, keepdims=True)
    return yc * lax.rsqrt(var + LN_EPS) * g + b


def _ada_kernel(c_ref, w_ref, b_ref, o_ref):
    c = c_ref[...]
    h = (c * jax.nn.sigmoid(c)).astype(BF16)
    acc = jnp.dot(h, w_ref[0].astype(BF16), preferred_element_type=F32)
    o_ref[0] = acc + b_ref[0]


def _ada(cond, w_ada, b_ada):
    depth, d, n = w_ada.shape
    tn = 1024
    est = 2 * d * tn * 4 + d * tn * 2 + MOD_ROWS * (d + 2 * tn) * 4
    return pl.pallas_call(
        _ada_kernel,
        grid=(depth, n // tn),
        in_specs=[
            pl.BlockSpec((MOD_ROWS, d), lambda i, j: (0, 0)),
            pl.BlockSpec((1, d, tn), lambda i, j: (i, 0, j)),
            pl.BlockSpec((1, 1, tn), lambda i, j: (i, 0, j)),
        ],
        out_specs=pl.BlockSpec((1, MOD_ROWS, tn), lambda i, j: (i, 0, j)),
        out_shape=jax.ShapeDtypeStruct((depth, MOD_ROWS, n), F32),
        compiler_params=_params(est, 2),
        name="ada_mod",
    )(cond, w_ada, b_ada)


def _rope(y, cos, sin_signed, first_half):
    half = HEAD_DIM // 2
    outs = []
    for g in range(y.shape[1] // V7X_LANES):
        yg = y[:, g * V7X_LANES:(g + 1) * V7X_LANES]
        partner = jnp.where(first_half,
                            pltpu.roll(yg, V7X_LANES - half, 1),
                            pltpu.roll(yg, half, 1))
        outs.append(yg * cos + partner * sin_signed)
    return outs


def _proj_kernel(*refs, n_out, rope, outs):
    x_ref, mod_ref = refs[0], refs[1]
    w_refs = refs[2:2 + n_out]
    pos = 2 + n_out
    if rope:
        cos_ref, sin_ref = refs[pos], refs[pos + 1]
        pos += 2
    o_refs = refs[pos:pos + n_out]
    d = x_ref.shape[-1]
    shift = mod_ref[0, :, 0:d]
    scale = mod_ref[0, :, d:2 * d]
    h = (x_ref[0] * (1.0 + scale) + shift).astype(BF16)
    if rope:
        cos = cos_ref[...]
        sin_signed = sin_ref[...]
        lane = lax.broadcasted_iota(jnp.int32, cos.shape, 1)
        first_half = (lane % HEAD_DIM) < (HEAD_DIM // 2)
    for j in range(n_out):
        y = jnp.dot(h, w_refs[j][...], preferred_element_type=F32)
        scale, rope_cols = outs[j]
        n_rot = rope_cols if rope else 0
        if n_rot:
            for g, part in enumerate(_rope(y[:, :n_rot], cos, sin_signed, first_half)):
                if scale != 1.0:
                    part = part * scale
                o_refs[j][0, :, g * V7X_LANES:(g + 1) * V7X_LANES] = part.astype(BF16)
        if n_rot < y.shape[1]:
            rest = y[:, n_rot:]
            if scale != 1.0 and n_rot == 0:
                rest = rest * scale
            o_refs[j][0, :, n_rot:] = rest.astype(BF16)


def _project(x, mod, weights, rope_tables, outs, tm):
    nb, rows, d = x.shape
    n_out = len(weights)
    rope = rope_tables is not None
    mod_map = (lambda b, i: (b, 0, 0)) if mod.shape[0] == nb and nb > 1 else (lambda b, i: (0, 0, 0))
    in_specs = [
        pl.BlockSpec((1, tm, d), lambda b, i: (b, i, 0)),
        pl.BlockSpec((1, 1, mod.shape[-1]), mod_map),
    ]
    for w in weights:
        in_specs.append(pl.BlockSpec(w.shape, lambda b, i: (0, 0)))
    args = [x, mod] + list(weights)
    if rope:
        in_specs += [pl.BlockSpec((tm, V7X_LANES), lambda b, i: (i, 0))] * 2
        args += list(rope_tables)
    n_tot = sum(w.shape[1] for w in weights)
    est = 2 * tm * d * 4 + 2 * d * n_tot * 2 + 2 * tm * n_tot * 2 + tm * d * 2 + 2 * tm * n_tot * 4
    return pl.pallas_call(
        functools.partial(_proj_kernel, n_out=n_out, rope=rope, outs=tuple(outs)),
        grid=(nb, rows // tm),
        in_specs=in_specs,
        out_specs=[pl.BlockSpec((1, tm, w.shape[1]), lambda b, i: (b, i, 0)) for w in weights],
        out_shape=[jax.ShapeDtypeStruct((nb, rows, w.shape[1]), BF16) for w in weights],
        compiler_params=_params(est, 2),
        name="mod_proj_rope" if rope else "mod_proj",
    )(*args)


def _gqa_block(q, keys, vals, masks, sink_ref, o_ref, row0):
    tq = q.shape[0]
    width = A_KV_HEADS * HEAD_DIM
    contract_last = (((1,), (1,)), ((), ()))
    qs = jnp.concatenate([q[:, h * width:(h + 1) * width] for h in range(A_GROUP)], axis=0)
    q_lane = lax.broadcasted_iota(jnp.int32, qs.shape, 1)
    vals_t = vals.astype(F32).T.astype(BF16)
    masks = [(r0, jnp.concatenate([valid] * A_GROUP, axis=1)) for r0, valid in masks]
    outs = []
    for g in range(A_KV_HEADS):
        qg = jnp.where((q_lane >= g * HEAD_DIM) & (q_lane < (g + 1) * HEAD_DIM), qs, jnp.zeros_like(qs))
        s = lax.dot_general(keys, qg, contract_last, preferred_element_type=F32)
        pieces, row = [], 0
        for r0, valid in masks:
            if r0 > row:
                pieces.append(s[row:r0])
            pieces.append(jnp.where(valid, s[r0:r0 + valid.shape[0]], NEG_INF))
            row = r0 + valid.shape[0]
        if masks:
            s = jnp.concatenate(pieces + [s[row:]], axis=0)
        sink = jnp.concatenate(
            [jnp.full((1, tq), sink_ref[g * A_GROUP + h] * LOG2_E, F32) for h in range(A_GROUP)], axis=1)
        m = jnp.maximum(jnp.max(s, axis=0, keepdims=True), sink)
        p = jnp.exp2(s - m)
        inv = 1.0 / (jnp.sum(p, axis=0, keepdims=True) + jnp.exp2(sink - m))
        acc = jnp.dot(vals_t[g * HEAD_DIM:(g + 1) * HEAD_DIM, :], p.astype(BF16), preferred_element_type=F32)
        outs.append(acc * inv)
    out_t = jnp.concatenate(outs, axis=0)
    for h in range(A_GROUP):
        o_ref[0, row0:row0 + tq, h * width:(h + 1) * width] = out_t[:, h * tq:(h + 1) * tq].T.astype(BF16)


def _window_attn_kernel(sink_ref, q_ref, *refs, q_blocks):
    n_kv = q_blocks + 2
    k_refs, v_refs = refs[:n_kv], refs[n_kv:2 * n_kv]
    kc_ref, vc_ref, o_ref = refs[2 * n_kv:]
    first = pl.program_id(1) * q_blocks
    last = pl.num_programs(1) * q_blocks - 1
    key = lax.broadcasted_iota(jnp.int32, (BLOCK, BLOCK), 0)
    qry = lax.broadcasted_iota(jnp.int32, (BLOCK, BLOCK), 1)
    for t in range(q_blocks):
        masks = [
            (0, (key >= qry) & (first + t > 0)),
            (2 * BLOCK, (key <= qry) & (first + t < last)),
        ]
        keys = jnp.concatenate([k_refs[t][0], k_refs[t + 1][0], k_refs[t + 2][0], kc_ref[0]], axis=0)
        vals = jnp.concatenate([v_refs[t][0], v_refs[t + 1][0], v_refs[t + 2][0], vc_ref[0]], axis=0)
        _gqa_block(q_ref[0, t * BLOCK:(t + 1) * BLOCK, :], keys, vals, masks, sink_ref, o_ref, t * BLOCK)


def _window_attention(q, kv, kvc, sink, q_blocks):
    b, l, dq = q.shape
    dk = kv.shape[-1] // 2
    c = kvc.shape[1]
    nblk = l // BLOCK
    tq = q_blocks * BLOCK

    def kv_spec(j, half):
        return pl.BlockSpec((1, BLOCK, dk),
                            lambda bi, n: (bi, jnp.clip(n * q_blocks - 1 + j, 0, nblk - 1), half))

    k_specs = [kv_spec(j, 0) for j in range(q_blocks + 2)]
    v_specs = [kv_spec(j, 1) for j in range(q_blocks + 2)]
    ctx_specs = [pl.BlockSpec((1, c, dk), lambda bi, n: (bi, 0, 0)), pl.BlockSpec((1, c, dk), lambda bi, n: (bi, 0, 1))]
    est = 4 * tq * dq * 2 + 8 * (tq + 2 * BLOCK + c) * dk * 2 + 6 * q_blocks * A_GROUP * BLOCK * (3 * BLOCK + c) * 4
    return pl.pallas_call(
        functools.partial(_window_attn_kernel, q_blocks=q_blocks),
        grid=(b, nblk // q_blocks),
        in_specs=[pl.BlockSpec(memory_space=pltpu.SMEM), pl.BlockSpec((1, tq, dq), lambda bi, n: (bi, n, 0))]
        + k_specs + v_specs + ctx_specs,
        out_specs=pl.BlockSpec((1, tq, dq), lambda bi, n: (bi, n, 0)),
        out_shape=jax.ShapeDtypeStruct((b, l, dq), BF16),
        compiler_params=_params(est, 2),
        name="window_gqa_attn",
    )(sink, q, *([kv] * (2 * (q_blocks + 2))), kvc, kvc)


def _ctx_attn_kernel(sink_ref, q_ref, kc_ref, vc_ref, o_ref):
    _gqa_block(q_ref[0], kc_ref[0], vc_ref[0], [], sink_ref, o_ref, 0)


def _ctx_attention(qc, kvc, sink):
    b, c, dq = qc.shape
    dk = kvc.shape[-1] // 2
    est = 4 * c * dq * 2 + 4 * c * dk * 2 + 6 * A_GROUP * c * c * 4
    return pl.pallas_call(
        _ctx_attn_kernel,
        grid=(b,),
        in_specs=[
            pl.BlockSpec(memory_space=pltpu.SMEM),
            pl.BlockSpec((1, c, dq), lambda bi: (bi, 0, 0)),
            pl.BlockSpec((1, c, dk), lambda bi: (bi, 0, 0)),
            pl.BlockSpec((1, c, dk), lambda bi: (bi, 0, 1)),
        ],
        out_specs=pl.BlockSpec((1, c, dq), lambda bi: (bi, 0, 0)),
        out_shape=jax.ShapeDtypeStruct((b, c, dq), BF16),
        compiler_params=_params(est, 1),
        name="ctx_gqa_attn",
    )(sink, qc, kvc, kvc)


def _diff_attn_kernel(lq1_ref, lk1_ref, lq2_ref, lk2_ref, g_ref, q_ref, k_ref, v_ref, kc_ref, vc_ref,
                      o_ref, k_all, v_all_t, *s_bufs, lam_init, tq, heads):
    seq = k_ref.shape[1]
    n_tiles = seq // tq
    hd = 2 * HEAD_DIM
    keys = k_all.shape[1]
    lam = (jnp.exp(jnp.sum(lq1_ref[...] * lk1_ref[...], axis=-1, keepdims=True))
           - jnp.exp(jnp.sum(lq2_ref[...] * lk2_ref[...], axis=-1, keepdims=True)) + lam_init)
    lane = lax.broadcasted_iota(jnp.int32, (tq, hd), 1)
    contract_last = (((1,), (1,)), ((), ()))
    out_gain = g_ref[...] * (1.0 - lam_init)

    pad_row = lax.broadcasted_iota(jnp.int32, (v_all_t.shape[1] - hd, keys), 0)
    for h in range(heads):
        cols = slice(h * hd, (h + 1) * hd)
        k_all[h, :seq, :] = k_ref[0, :, cols]
        k_all[h, seq:, :] = kc_ref[0, :, cols]
        v_all_t[h, :hd, :seq] = v_ref[0, :, cols].astype(F32).T.astype(BF16)
        v_all_t[h, :hd, seq:] = vc_ref[0, :, cols].astype(F32).T.astype(BF16)
        v_all_t[h, hd:, :] = jnp.where(pad_row == 0, 1.0, 0.0).astype(BF16)

    def scores(h, i, s_ref):
        q = q_ref[0, i * tq:(i + 1) * tq, h * hd:(h + 1) * hd]
        zero = jnp.zeros_like(q)
        qq = jnp.concatenate([jnp.where(lane < HEAD_DIM, q, zero), jnp.where(lane >= HEAD_DIM, q, zero)], axis=0)
        s_ref[...] = lax.dot_general(k_all[h], qq, contract_last, preferred_element_type=F32)

    def finish(h, i, s_ref):
        s = s_ref[...]
        p = jnp.exp2(s - jnp.max(s, axis=0, keepdims=True)).astype(BF16)
        acc = jnp.dot(v_all_t[h], p, preferred_element_type=F32)
        n0, n1 = acc[hd:hd + 1, :tq], acc[hd:hd + 1, tq:]
        o_t = acc[:hd, :tq] / n0 - acc[:hd, tq:] * (lam / n1)
        o_t = o_t * lax.rsqrt(jnp.mean(o_t * o_t, axis=0, keepdims=True) + SUBLN_EPS)
        o_ref[0, i * tq:(i + 1) * tq, h * hd:(h + 1) * hd] = (o_t * out_gain).T.astype(BF16)

    work = [(h, i) for i in range(n_tiles) for h in range(heads)]
    scores(*work[0], s_bufs[0])
    for n, (h, i) in enumerate(work):
        if n + 1 < len(work):
            scores(*work[n + 1], s_bufs[(n + 1) % len(s_bufs)])
        finish(h, i, s_bufs[n % len(s_bufs)])


def _diff_attention(q, k, v, kc, vc, lq1, lk1, lq2, lk2, subln_g, lam_init, tq, heads):
    b, l, dm = q.shape
    c = kc.shape[1]
    hd = 2 * HEAD_DIM
    vec = lambda n: pl.BlockSpec((1, n), lambda bi, h: (0, 0))
    per_group = lambda rows: pl.BlockSpec((1, rows, heads * hd), lambda bi, h: (bi, 0, h))
    score_buf = pltpu.VMEM((l + c, 2 * tq), F32)
    est = 2 * heads * (3 * l + 2 * c) * hd * 2 * 2 + 6 * 2 * tq * (l + c) * 4
    return pl.pallas_call(
        functools.partial(_diff_attn_kernel, lam_init=lam_init, tq=tq, heads=heads),
        grid=(b, dm // (heads * hd)),
        in_specs=[
            vec(HEAD_DIM), vec(HEAD_DIM), vec(HEAD_DIM), vec(HEAD_DIM),
            pl.BlockSpec((hd, 1), lambda bi, h: (0, 0)),
            per_group(l), per_group(l), per_group(l), per_group(c), per_group(c),
        ],
        out_specs=per_group(l),
        out_shape=jax.ShapeDtypeStruct((b, l, dm), BF16),
        scratch_shapes=[pltpu.VMEM((heads, l + c, hd), BF16),
                        pltpu.VMEM((heads, hd + BF16_SUBLANES, l + c), BF16), score_buf, score_buf],
        compiler_params=_params(est, 2),
        name="diff_attn",
    )(lq1, lk1, lq2, lk2, subln_g, q, k, v, kc, vc)


def _post_kernel(a_ref, x_ref, mod_ref, wo_ref, w1_ref, w2_ref, ln_ref, o_ref, *, ff_chunk):
    d = x_ref.shape[-1]
    gate1 = mod_ref[0, :, 2 * d:3 * d]
    shift2 = mod_ref[0, :, 3 * d:4 * d]
    scale2 = mod_ref[0, :, 4 * d:5 * d]
    gate2 = mod_ref[0, :, 5 * d:6 * d]
    ax = jnp.dot(a_ref[0], wo_ref[...], preferred_element_type=F32)
    x1 = _layer_norm(DEEPNORM_ALPHA * x_ref[0] + gate1 * ax, ln_ref[0:1, :], ln_ref[1:2, :])
    h = (x1 * (1.0 + scale2) + shift2).astype(BF16)
    fx = jnp.zeros_like(x1)
    for j in range(w1_ref.shape[1] // ff_chunk):
        u = jnp.dot(h, w1_ref[:, j * ff_chunk:(j + 1) * ff_chunk], preferred_element_type=F32)
        u = jnp.maximum(u, 0.0)
        fx = fx + jnp.dot((u * u).astype(BF16), w2_ref[j * ff_chunk:(j + 1) * ff_chunk, :],
                          preferred_element_type=F32)
    o_ref[0] = _layer_norm(DEEPNORM_ALPHA * x1 + gate2 * fx, ln_ref[2:3, :], ln_ref[3:4, :])


def _post(attn, x, mod, wo, w1, w2, ln, tm):
    nb, rows, d = x.shape
    dff = w1.shape[1]
    ff_chunk = 1024
    mod_map = (lambda b, i: (b, 0, 0)) if mod.shape[0] == nb and nb > 1 else (lambda b, i: (0, 0, 0))
    const = lambda shape: pl.BlockSpec(shape, lambda b, i: (0, 0), pipeline_mode=pl.Buffered(1))
    est = ((d * d + 2 * d * dff) * 2 + 2 * tm * d * (2 + 4 + 4)
           + tm * d * 4 * 4 + tm * ff_chunk * 6)
    return pl.pallas_call(
        functools.partial(_post_kernel, ff_chunk=ff_chunk),
        grid=(nb, rows // tm),
        in_specs=[
            pl.BlockSpec((1, tm, d), lambda b, i: (b, i, 0)),
            pl.BlockSpec((1, tm, d), lambda b, i: (b, i, 0)),
            pl.BlockSpec((1, 1, mod.shape[-1]), mod_map),
            const(wo.shape), const(w1.shape), const(w2.shape),
            pl.BlockSpec(ln.shape, lambda b, i: (0, 0)),
        ],
        out_specs=pl.BlockSpec((1, tm, d), lambda b, i: (b, i, 0)),
        out_shape=jax.ShapeDtypeStruct((nb, rows, d), F32),
        compiler_params=_params(est, 2),
        name="oproj_ln_mlp_ln",
    )(attn, x, mod, wo, w1, w2, ln)


def _rope_tables(seq):
    rows = seq // GRID_W
    row = jnp.repeat(jnp.arange(rows, dtype=F32), GRID_W)
    col = jnp.tile(jnp.arange(GRID_W, dtype=F32), rows)
    n_freq = HEAD_DIM // 4
    inv = ROPE_BASE ** (-jnp.arange(n_freq, dtype=F32) / n_freq)
    ang = jnp.concatenate([row[:, None] * inv, col[:, None] * inv], axis=-1)
    cos, sin = jnp.cos(ang), jnp.sin(ang)
    reps = V7X_LANES // HEAD_DIM
    cos_t = jnp.tile(jnp.concatenate([cos, cos], axis=-1), (1, reps))
    sin_t = jnp.tile(jnp.concatenate([-sin, sin], axis=-1), (1, reps))
    return cos_t, sin_t


def kernel(x, c, ctx, c_ctx, w_ada, b_ada, ln1_g, ln1_b, ln2_g, ln2_b, a_wq, a_wk, a_wv, a_wo, a_sink,
           b_wq, b_wk, b_wv, b_wo, b_lq1, b_lk1, b_lq2, b_lk2, b_subln_g, mlp_w1, mlp_w2):
    batch, seq, d = x.shape
    ctx_len = ctx.shape[1]
    tables = _rope_tables(seq)
    bf = lambda w: w.astype(BF16)

    cond = jnp.concatenate([c, c_ctx[None, :], jnp.zeros((MOD_ROWS - batch - 1, d), F32)], axis=0)
    mod = _ada(cond, w_ada, b_ada[:, None, :])

    ctx_rows = 512

    def ctx_project(ctx, mod_c, weights, outs):
        ys = _project(ctx.reshape(1, batch * ctx_len, d), mod_c, weights, None, outs, tm=ctx_rows)
        return [y.reshape(batch, ctx_len, y.shape[-1]) for y in ys]

    q_out = (QK_SCALE * LOG2_E, d)
    plain = (1.0, 0)

    for i in range(DEPTH):
        need_ctx = i < DEPTH - 1
        mod_x = mod[i, :batch][:, None, :]
        mod_c = mod[i, batch:batch + 1][:, None, :]
        ln = jnp.stack([ln1_g[i], ln1_b[i], ln2_g[i], ln2_b[i]], axis=0)
        j = i // N_MIXERS
        if i % N_MIXERS == 0:
            wq = a_wq[j].reshape(d, A_KV_HEADS, A_GROUP, HEAD_DIM).swapaxes(1, 2).reshape(d, d)
            wo = a_wo[j].reshape(A_KV_HEADS, A_GROUP, HEAD_DIM, d).swapaxes(0, 1).reshape(d, d)
            wkv = jnp.concatenate([a_wk[j], a_wv[j]], axis=1)
            n_k = a_wk[j].shape[1]
            wq, wkv, wo = bf(wq), bf(wkv), bf(wo)
            q, kv = _project(x, mod_x, [wq, wkv], tables, [q_out, (1.0, n_k)], tm=512)
            if need_ctx:
                qc, kvc = ctx_project(ctx, mod_c, [wq, wkv], [q_out, plain])
            else:
                kvc, = ctx_project(ctx, mod_c, [wkv], [plain])
            attn_x = _window_attention(q, kv, kvc, a_sink[j], q_blocks=4)
            if need_ctx:
                attn_c = _ctx_attention(qc, kvc, a_sink[j])
        else:
            lam_init = 0.8 - 0.6 * math.exp(-0.3 * i)
            wq, wk, wv, wo = bf(b_wq[j]), bf(b_wk[j]), bf(b_wv[j]), bf(b_wo[j])
            q, k, v = _project(x, mod_x, [wq, wk, wv], tables, [q_out, (1.0, d), plain], tm=512)
            if need_ctx:
                raise NotImplementedError("context queries through differential attention")
            kc, vc = ctx_project(ctx, mod_c, [wk, wv], [plain, plain])
            attn_x = _diff_attention(q, k, v, kc, vc, b_lq1[j][None], b_lk1[j][None], b_lq2[j][None],
                                     b_lk2[j][None], b_subln_g[j][:, None], lam_init, tq=512, heads=2)
        w1, w2 = bf(mlp_w1[i]), bf(mlp_w2[i])
        x = _post(attn_x, x, mod_x, wo, w1, w2, ln, tm=512)
        if need_ctx:
            flat = (1, batch * ctx_len, d)
            ctx = _post(attn_c.reshape(flat), ctx.reshape(flat), mod_c, wo, w1, w2, ln, tm=ctx_rows)
            ctx = ctx.reshape(batch, ctx_len, d)
    return x
```

```python
import functools
import math

import jax
import jax.numpy as jnp
from jax import lax
from jax.experimental import pallas as pl
from jax.experimental.pallas import tpu as pltpu

D_MODEL = 1024
DEPTH = 2
GRID_W = 64
N_MIXERS = 2
BLOCK = 128
HEAD_DIM = 64
A_HEADS = D_MODEL // HEAD_DIM
A_KV_HEADS = max(1, A_HEADS // 8)
A_GROUP = A_HEADS // A_KV_HEADS
B_HEADS = D_MODEL // (2 * HEAD_DIM)
D_FF = 4 * D_MODEL
N_MOD = 6
ROPE_BASE = 10000.0
LN_EPS = 1e-5
SUBLN_EPS = 1e-5
NEG_INF = -1e30
DEEPNORM_ALPHA = (2 * DEPTH) ** 0.25
QK_SCALE = HEAD_DIM ** -0.5
LOG2_E = math.log2(math.e)

V7X_LANES = 128
BF16_SUBLANES = 16
V7X_VMEM_BYTES = 64 * 1024 * 1024
MOD_ROWS = 24

BF16 = jnp.bfloat16
F32 = jnp.float32


def _vmem_limit(estimate_bytes):
    return int(min(max(2 * estimate_bytes, 16 * 1024 * 1024), V7X_VMEM_BYTES - 8 * 1024 * 1024))


def _params(estimate_bytes, n_grid):
    return pltpu.CompilerParams(
        dimension_semantics=("arbitrary",) * n_grid,
        vmem_limit_bytes=_vmem_limit(estimate_bytes),
    )


def _pair_member(lane):
    return (lane % HEAD_DIM) // (HEAD_DIM // 2)


def _pair_interleave(w):
    rows, cols = w.shape
    w = w.reshape(rows, cols // V7X_LANES, 2, 2, HEAD_DIM // 2)
    return w.swapaxes(2, 3).reshape(rows, cols)


def _layer_norm(y, g, b):
    mu = jnp.mean(y, axis=-1, keepdims=True)
    yc = y - mu
    var = jnp.mean(yc * yc, axis=-1, keepdims=True)
    return yc * lax.rsqrt(var + LN_EPS) * g + b


def _ada_kernel(c_ref, w_ref, b_ref, o_ref):
    c = c_ref[...]
    h = (c * jax.nn.sigmoid(c)).astype(BF16)
    acc = jnp.dot(h, w_ref[0].astype(BF16), preferred_element_type=F32)
    o_ref[0] = acc + b_ref[0]


def _ada(cond, w_ada, b_ada):
    depth, d, n = w_ada.shape
    tn = 1024
    est = 2 * d * tn * 4 + d * tn * 2 + MOD_ROWS * (d + 2 * tn) * 4
    return pl.pallas_call(
        _ada_kernel,
        grid=(depth, n // tn),
        in_specs=[
            pl.BlockSpec((MOD_ROWS, d), lambda i, j: (0, 0)),
            pl.BlockSpec((1, d, tn), lambda i, j: (i, 0, j)),
            pl.BlockSpec((1, 1, tn), lambda i, j: (i, 0, j)),
        ],
        out_specs=pl.BlockSpec((1, MOD_ROWS, tn), lambda i, j: (i, 0, j)),
        out_shape=jax.ShapeDtypeStruct((depth, MOD_ROWS, n), F32),
        compiler_params=_params(est, 2),
        name="ada_mod",
    )(cond, w_ada, b_ada)


def _rope(y, cos, sin_signed):
    outs = []
    for g in range(y.shape[1] // V7X_LANES):
        yg = y[:, g * V7X_LANES:(g + 1) * V7X_LANES]
        outs.append(yg * cos + pltpu.roll(yg, V7X_LANES // 2, 1) * sin_signed)
    return outs


def _proj_kernel(*refs, n_out, rope, scales):
    x_ref, mod_ref = refs[0], refs[1]
    w_refs = refs[2:2 + n_out]
    pos = 2 + n_out
    if rope:
        cos_ref, sin_ref = refs[pos], refs[pos + 1]
        pos += 2
    o_refs = refs[pos:pos + n_out]
    d = x_ref.shape[-1]
    shift = mod_ref[0, :, 0:d]
    scale = mod_ref[0, :, d:2 * d]
    h = (x_ref[0] * (1.0 + scale) + shift).astype(BF16)
    if rope:
        cos = cos_ref[...]
        sin_signed = sin_ref[...]
    for j in range(n_out):
        y = jnp.dot(h, w_refs[j][...], preferred_element_type=F32)
        roped = rope and scales[j] is not None
        if roped:
            parts = _rope(y, cos, sin_signed)
            for g, part in enumerate(parts):
                if scales[j] != 1.0:
                    part = part * scales[j]
                o_refs[j][0, :, g * V7X_LANES:(g + 1) * V7X_LANES] = part.astype(BF16)
        else:
            if scales[j] is not None and scales[j] != 1.0:
                y = y * scales[j]
            o_refs[j][0] = y.astype(BF16)


def _project(x, mod, weights, rope_tables, scales, tm):
    nb, rows, d = x.shape
    n_out = len(weights)
    rope = rope_tables is not None
    mod_map = (lambda b, i: (b, 0, 0)) if mod.shape[0] == nb and nb > 1 else (lambda b, i: (0, 0, 0))
    in_specs = [
        pl.BlockSpec((1, tm, d), lambda b, i: (b, i, 0)),
        pl.BlockSpec((1, 1, mod.shape[-1]), mod_map),
    ]
    for w in weights:
        in_specs.append(pl.BlockSpec(w.shape, lambda b, i: (0, 0)))
    args = [x, mod] + list(weights)
    if rope:
        in_specs += [pl.BlockSpec((tm, V7X_LANES), lambda b, i: (i, 0))] * 2
        args += list(rope_tables)
    n_tot = sum(w.shape[1] for w in weights)
    est = 2 * tm * d * 4 + 2 * d * n_tot * 2 + 2 * tm * n_tot * 2 + tm * d * 2 + 2 * tm * n_tot * 4
    return pl.pallas_call(
        functools.partial(_proj_kernel, n_out=n_out, rope=rope, scales=tuple(scales)),
        grid=(nb, rows // tm),
        in_specs=in_specs,
        out_specs=[pl.BlockSpec((1, tm, w.shape[1]), lambda b, i: (b, i, 0)) for w in weights],
        out_shape=[jax.ShapeDtypeStruct((nb, rows, w.shape[1]), BF16) for w in weights],
        compiler_params=_params(est, 2),
        name="mod_proj_rope" if rope else "mod_proj",
    )(*args)


def _gqa_block(q, keys, vals, masks, sink_ref, o_ref, row0):
    tq = q.shape[0]
    width = A_KV_HEADS * HEAD_DIM
    contract_last = (((1,), (1,)), ((), ()))
    qs = jnp.concatenate([q[:, h * width:(h + 1) * width] for h in range(A_GROUP)], axis=0)
    q_member = _pair_member(lax.broadcasted_iota(jnp.int32, qs.shape, 1))
    vals_t = vals.astype(F32).T.astype(BF16)
    masks = [(r0, jnp.concatenate([valid] * A_GROUP, axis=1)) for r0, valid in masks]
    outs = []
    for g in range(A_KV_HEADS):
        qg = jnp.where(q_member == g, qs, jnp.zeros_like(qs))
        s = lax.dot_general(keys, qg, contract_last, preferred_element_type=F32)
        pieces, row = [], 0
        for r0, valid in masks:
            if r0 > row:
                pieces.append(s[row:r0])
            pieces.append(jnp.where(valid, s[r0:r0 + valid.shape[0]], NEG_INF))
            row = r0 + valid.shape[0]
        if masks:
            s = jnp.concatenate(pieces + [s[row:]], axis=0)
        sink = jnp.concatenate(
            [jnp.full((1, tq), sink_ref[g * A_GROUP + h] * LOG2_E, F32) for h in range(A_GROUP)], axis=1)
        m = jnp.maximum(jnp.max(s, axis=0, keepdims=True), sink)
        p = jnp.exp2(s - m)
        inv = 1.0 / (jnp.sum(p, axis=0, keepdims=True) + jnp.exp2(sink - m))
        acc = jnp.dot(vals_t[g * HEAD_DIM:(g + 1) * HEAD_DIM, :], p.astype(BF16), preferred_element_type=F32)
        outs.append(acc * inv)
    out_t = jnp.concatenate(outs, axis=0)
    for h in range(A_GROUP):
        o_ref[0, row0:row0 + tq, h * width:(h + 1) * width] = out_t[:, h * tq:(h + 1) * tq].T.astype(BF16)


def _window_attn_kernel(sink_ref, q_ref, *refs, q_blocks):
    n_kv = q_blocks + 2
    k_refs, v_refs = refs[:n_kv], refs[n_kv:2 * n_kv]
    kc_ref, vc_ref, o_ref = refs[2 * n_kv:]
    first = pl.program_id(1) * q_blocks
    last = pl.num_programs(1) * q_blocks - 1
    key = lax.broadcasted_iota(jnp.int32, (BLOCK, BLOCK), 0)
    qry = lax.broadcasted_iota(jnp.int32, (BLOCK, BLOCK), 1)
    for t in range(q_blocks):
        masks = [
            (0, (key >= qry) & (first + t > 0)),
            (2 * BLOCK, (key <= qry) & (first + t < last)),
        ]
        keys = jnp.concatenate([k_refs[t][0], k_refs[t + 1][0], k_refs[t + 2][0], kc_ref[0]], axis=0)
        vals = jnp.concatenate([v_refs[t][0], v_refs[t + 1][0], v_refs[t + 2][0], vc_ref[0]], axis=0)
        _gqa_block(q_ref[0, t * BLOCK:(t + 1) * BLOCK, :], keys, vals, masks, sink_ref, o_ref, t * BLOCK)


def _window_attention(q, k, v, kc, vc, sink, q_blocks):
    b, l, dq = q.shape
    dk = k.shape[-1]
    c = kc.shape[1]
    nblk = l // BLOCK
    tq = q_blocks * BLOCK

    def kv_spec(j):
        return pl.BlockSpec((1, BLOCK, dk), lambda bi, n: (bi, jnp.clip(n * q_blocks - 1 + j, 0, nblk - 1), 0))

    kv_specs = [kv_spec(j) for j in range(q_blocks + 2)]
    ctx_spec = pl.BlockSpec((1, c, dk), lambda bi, n: (bi, 0, 0))
    est = 4 * tq * dq * 2 + 8 * (tq + 2 * BLOCK + c) * dk * 2 + 6 * q_blocks * A_GROUP * BLOCK * (3 * BLOCK + c) * 4
    return pl.pallas_call(
        functools.partial(_window_attn_kernel, q_blocks=q_blocks),
        grid=(b, nblk // q_blocks),
        in_specs=[pl.BlockSpec(memory_space=pltpu.SMEM), pl.BlockSpec((1, tq, dq), lambda bi, n: (bi, n, 0))]
        + kv_specs + kv_specs + [ctx_spec, ctx_spec],
        out_specs=pl.BlockSpec((1, tq, dq), lambda bi, n: (bi, n, 0)),
        out_shape=jax.ShapeDtypeStruct((b, l, dq), BF16),
        compiler_params=_params(est, 2),
        name="window_gqa_attn",
    )(sink, q, *([k] * (q_blocks + 2)), *([v] * (q_blocks + 2)), kc, vc)


def _ctx_attn_kernel(sink_ref, q_ref, kc_ref, vc_ref, o_ref):
    _gqa_block(q_ref[0], kc_ref[0], vc_ref[0], [], sink_ref, o_ref, 0)


def _ctx_attention(qc, kc, vc, sink):
    b, c, dq = qc.shape
    dk = kc.shape[-1]
    est = 4 * c * dq * 2 + 4 * c * dk * 2 + 6 * A_GROUP * c * c * 4
    return pl.pallas_call(
        _ctx_attn_kernel,
        grid=(b,),
        in_specs=[
            pl.BlockSpec(memory_space=pltpu.SMEM),
            pl.BlockSpec((1, c, dq), lambda bi: (bi, 0, 0)),
            pl.BlockSpec((1, c, dk), lambda bi: (bi, 0, 0)),
            pl.BlockSpec((1, c, dk), lambda bi: (bi, 0, 0)),
        ],
        out_specs=pl.BlockSpec((1, c, dq), lambda bi: (bi, 0, 0)),
        out_shape=jax.ShapeDtypeStruct((b, c, dq), BF16),
        compiler_params=_params(est, 1),
        name="ctx_gqa_attn",
    )(sink, qc, kc, vc)


def _diff_attn_kernel(lq1_ref, lk1_ref, lq2_ref, lk2_ref, g_ref, q_ref, k_ref, v_ref, kc_ref, vc_ref,
                      o_ref, k_all, v_all_t, *s_bufs, lam_init, tq, heads):
    seq = k_ref.shape[1]
    n_tiles = seq // tq
    hd = 2 * HEAD_DIM
    keys = k_all.shape[1]
    lam = (jnp.exp(jnp.sum(lq1_ref[...] * lk1_ref[...], axis=-1, keepdims=True))
           - jnp.exp(jnp.sum(lq2_ref[...] * lk2_ref[...], axis=-1, keepdims=True)) + lam_init)
    member = _pair_member(lax.broadcasted_iota(jnp.int32, (tq, hd), 1))
    contract_last = (((1,), (1,)), ((), ()))
    out_gain = g_ref[...] * (1.0 - lam_init)

    pad_row = lax.broadcasted_iota(jnp.int32, (v_all_t.shape[1] - hd, keys), 0)
    for h in range(heads):
        cols = slice(h * hd, (h + 1) * hd)
        k_all[h, :seq, :] = k_ref[0, :, cols]
        k_all[h, seq:, :] = kc_ref[0, :, cols]
        v_all_t[h, :hd, :seq] = v_ref[0, :, cols].astype(F32).T.astype(BF16)
        v_all_t[h, :hd, seq:] = vc_ref[0, :, cols].astype(F32).T.astype(BF16)
        v_all_t[h, hd:, :] = jnp.where(pad_row == 0, 1.0, 0.0).astype(BF16)

    def scores(h, i, s_ref):
        q = q_ref[0, i * tq:(i + 1) * tq, h * hd:(h + 1) * hd]
        zero = jnp.zeros_like(q)
        qq = jnp.concatenate([jnp.where(member == 0, q, zero), jnp.where(member == 1, q, zero)], axis=0)
        s_ref[...] = lax.dot_general(k_all[h], qq, contract_last, preferred_element_type=F32)

    def finish(h, i, s_ref):
        s = s_ref[...]
        p = jnp.exp2(s - jnp.max(s, axis=0, keepdims=True)).astype(BF16)
        acc = jnp.dot(v_all_t[h], p, preferred_element_type=F32)
        n0, n1 = acc[hd:hd + 1, :tq], acc[hd:hd + 1, tq:]
        o_t = acc[:hd, :tq] / n0 - acc[:hd, tq:] * (lam / n1)
        o_t = o_t * lax.rsqrt(jnp.mean(o_t * o_t, axis=0, keepdims=True) + SUBLN_EPS)
        o_ref[0, i * tq:(i + 1) * tq, h * hd:(h + 1) * hd] = (o_t * out_gain).T.astype(BF16)

    work = [(h, i) for i in range(n_tiles) for h in range(heads)]
    scores(*work[0], s_bufs[0])
    for n, (h, i) in enumerate(work):
        if n + 1 < len(work):
            scores(*work[n + 1], s_bufs[(n + 1) % len(s_bufs)])
        finish(h, i, s_bufs[n % len(s_bufs)])


def _diff_attention(q, k, v, kc, vc, lq1, lk1, lq2, lk2, subln_g, lam_init, tq, heads):
    b, l, dm = q.shape
    c = kc.shape[1]
    hd = 2 * HEAD_DIM
    vec = lambda n: pl.BlockSpec((1, n), lambda bi, h: (0, 0))
    per_group = lambda rows: pl.BlockSpec((1, rows, heads * hd), lambda bi, h: (bi, 0, h))
    score_buf = pltpu.VMEM((l + c, 2 * tq), F32)
    est = 2 * heads * (3 * l + 2 * c) * hd * 2 * 2 + 6 * 2 * tq * (l + c) * 4
    return pl.pallas_call(
        functools.partial(_diff_attn_kernel, lam_init=lam_init, tq=tq, heads=heads),
        grid=(b, dm // (heads * hd)),
        in_specs=[
            vec(HEAD_DIM), vec(HEAD_DIM), vec(HEAD_DIM), vec(HEAD_DIM),
            pl.BlockSpec((hd, 1), lambda bi, h: (0, 0)),
            per_group(l), per_group(l), per_group(l), per_group(c), per_group(c),
        ],
        out_specs=per_group(l),
        out_shape=jax.ShapeDtypeStruct((b, l, dm), BF16),
        scratch_shapes=[pltpu.VMEM((heads, l + c, hd), BF16),
                        pltpu.VMEM((heads, hd + BF16_SUBLANES, l + c), BF16), score_buf, score_buf],
        compiler_params=_params(est, 2),
        name="diff_attn",
    )(lq1, lk1, lq2, lk2, subln_g, q, k, v, kc, vc)


def _post_kernel(a_ref, x_ref, mod_ref, wo_ref, w1_ref, w2_ref, ln_ref, o_ref, *, ff_chunk):
    d = x_ref.shape[-1]
    gate1 = mod_ref[0, :, 2 * d:3 * d]
    shift2 = mod_ref[0, :, 3 * d:4 * d]
    scale2 = mod_ref[0, :, 4 * d:5 * d]
    gate2 = mod_ref[0, :, 5 * d:6 * d]
    ax = jnp.dot(a_ref[0], wo_ref[...], preferred_element_type=F32)
    x1 = _layer_norm(DEEPNORM_ALPHA * x_ref[0] + gate1 * ax, ln_ref[0:1, :], ln_ref[1:2, :])
    h = (x1 * (1.0 + scale2) + shift2).astype(BF16)
    fx = jnp.zeros_like(x1)
    for j in range(w1_ref.shape[1] // ff_chunk):
        u = jnp.dot(h, w1_ref[:, j * ff_chunk:(j + 1) * ff_chunk], preferred_element_type=F32)
        u = jnp.maximum(u, 0.0)
        fx = fx + jnp.dot((u * u).astype(BF16), w2_ref[j * ff_chunk:(j + 1) * ff_chunk, :],
                          preferred_element_type=F32)
    o_ref[0] = _layer_norm(DEEPNORM_ALPHA * x1 + gate2 * fx, ln_ref[2:3, :], ln_ref[3:4, :])


def _post(attn, x, mod, wo, w1, w2, ln, tm):
    nb, rows, d = x.shape
    dff = w1.shape[1]
    ff_chunk = 1024
    mod_map = (lambda b, i: (b, 0, 0)) if mod.shape[0] == nb and nb > 1 else (lambda b, i: (0, 0, 0))
    const = lambda shape: pl.BlockSpec(shape, lambda b, i: (0, 0), pipeline_mode=pl.Buffered(1))
    est = ((d * d + 2 * d * dff) * 2 + 2 * tm * d * (2 + 4 + 4)
           + tm * d * 4 * 4 + tm * ff_chunk * 6)
    return pl.pallas_call(
        functools.partial(_post_kernel, ff_chunk=ff_chunk),
        grid=(nb, rows // tm),
        in_specs=[
            pl.BlockSpec((1, tm, d), lambda b, i: (b, i, 0)),
            pl.BlockSpec((1, tm, d), lambda b, i: (b, i, 0)),
            pl.BlockSpec((1, 1, mod.shape[-1]), mod_map),
            const(wo.shape), const(w1.shape), const(w2.shape),
            pl.BlockSpec(ln.shape, lambda b, i: (0, 0)),
        ],
        out_specs=pl.BlockSpec((1, tm, d), lambda b, i: (b, i, 0)),
        out_shape=jax.ShapeDtypeStruct((nb, rows, d), F32),
        compiler_params=_params(est, 2),
        name="oproj_ln_mlp_ln",
    )(attn, x, mod, wo, w1, w2, ln)


def _rope_tables(seq):
    rows = seq // GRID_W
    row = jnp.repeat(jnp.arange(rows, dtype=F32), GRID_W)
    col = jnp.tile(jnp.arange(GRID_W, dtype=F32), rows)
    n_freq = HEAD_DIM // 4
    inv = ROPE_BASE ** (-jnp.arange(n_freq, dtype=F32) / n_freq)
    ang = jnp.concatenate([row[:, None] * inv, col[:, None] * inv], axis=-1)
    cos, sin = jnp.cos(ang), jnp.sin(ang)
    cos_t = jnp.concatenate([cos, cos, cos, cos], axis=-1)
    sin_t = jnp.concatenate([-sin, -sin, sin, sin], axis=-1)
    return cos_t, sin_t


def kernel(x, c, ctx, c_ctx, w_ada, b_ada, ln1_g, ln1_b, ln2_g, ln2_b, a_wq, a_wk, a_wv, a_wo, a_sink,
           b_wq, b_wk, b_wv, b_wo, b_lq1, b_lk1, b_lq2, b_lk2, b_subln_g, mlp_w1, mlp_w2):
    batch, seq, d = x.shape
    ctx_len = ctx.shape[1]
    tables = _rope_tables(seq)
    bf = lambda w: w.astype(BF16)

    cond = jnp.concatenate([c, c_ctx[None, :], jnp.zeros((MOD_ROWS - batch - 1, d), F32)], axis=0)
    mod = _ada(cond, w_ada, b_ada[:, None, :])

    ctx_rows = 512

    def ctx_project(ctx, mod_c, weights, scales):
        outs = _project(ctx.reshape(1, batch * ctx_len, d), mod_c, weights, None, scales, tm=ctx_rows)
        return [o.reshape(batch, ctx_len, o.shape[-1]) for o in outs]

    for i in range(DEPTH):
        need_ctx = i < DEPTH - 1
        mod_x = mod[i, :batch][:, None, :]
        mod_c = mod[i, batch:batch + 1][:, None, :]
        ln = jnp.stack([ln1_g[i], ln1_b[i], ln2_g[i], ln2_b[i]], axis=0)
        j = i // N_MIXERS
        if i % N_MIXERS == 0:
            wq = a_wq[j].reshape(d, A_KV_HEADS, A_GROUP, HEAD_DIM).swapaxes(1, 2).reshape(d, d)
            wo = a_wo[j].reshape(A_KV_HEADS, A_GROUP, HEAD_DIM, d).swapaxes(0, 1).reshape(d, d)
            wq, wk, wv, wo = bf(_pair_interleave(wq)), bf(_pair_interleave(a_wk[j])), bf(a_wv[j]), bf(wo)
            q, k, v = _project(x, mod_x, [wq, wk, wv], tables, [QK_SCALE * LOG2_E, 1.0, None], tm=512)
            if need_ctx:
                qc, kc, vc = ctx_project(ctx, mod_c, [wq, wk, wv], [QK_SCALE * LOG2_E, None, None])
            else:
                kc, vc = ctx_project(ctx, mod_c, [wk, wv], [None, None])
            attn_x = _window_attention(q, k, v, kc, vc, a_sink[j], q_blocks=4)
            if need_ctx:
                attn_c = _ctx_attention(qc, kc, vc, a_sink[j])
        else:
            lam_init = 0.8 - 0.6 * math.exp(-0.3 * i)
            wq, wk = bf(_pair_interleave(b_wq[j])), bf(_pair_interleave(b_wk[j]))
            wv, wo = bf(b_wv[j]), bf(b_wo[j])
            q, k, v = _project(x, mod_x, [wq, wk, wv], tables, [QK_SCALE * LOG2_E, 1.0, None], tm=512)
            if need_ctx:
                raise NotImplementedError("context queries through differential attention")
            kc, vc = ctx_project(ctx, mod_c, [wk, wv], [None, None])
            attn_x = _diff_attention(q, k, v, kc, vc, b_lq1[j][None], b_lk1[j][None], b_lq2[j][None],
                                     b_lk2[j][None], b_subln_g[j][:, None], lam_init, tq=512, heads=2)
        w1, w2 = bf(mlp_w1[i]), bf(mlp_w2[i])
        x = _post(attn_x, x, mod_x, wo, w1, w2, ln, tm=512)
        if need_ctx:
            flat = (1, batch * ctx_len, d)
            ctx = _post(attn_c.reshape(flat), ctx.reshape(flat), mod_c, wo, w1, w2, ln, tm=ctx_rows)
            ctx = ctx.reshape(batch, ctx_len, d)
    return x
```

```python
import functools
import math

import jax
import jax.numpy as jnp
from jax import lax
from jax.experimental import pallas as pl
from jax.experimental.pallas import tpu as pltpu

D_MODEL = 1024
DEPTH = 2
GRID_W = 64
N_MIXERS = 2
BLOCK = 128
HEAD_DIM = 64
A_HEADS = D_MODEL // HEAD_DIM
A_KV_HEADS = max(1, A_HEADS // 8)
A_GROUP = A_HEADS // A_KV_HEADS
B_HEADS = D_MODEL // (2 * HEAD_DIM)
D_FF = 4 * D_MODEL
N_MOD = 6
ROPE_BASE = 10000.0
LN_EPS = 1e-5
SUBLN_EPS = 1e-5
NEG_INF = -1e30
DEEPNORM_ALPHA = (2 * DEPTH) ** 0.25
QK_SCALE = HEAD_DIM ** -0.5
LOG2_E = math.log2(math.e)

V7X_LANES = 128
BF16_SUBLANES = 16
V7X_VMEM_BYTES = 64 * 1024 * 1024
MOD_ROWS = 24

BF16 = jnp.bfloat16
F32 = jnp.float32


def _vmem_limit(estimate_bytes):
    return int(min(max(2 * estimate_bytes, 16 * 1024 * 1024), V7X_VMEM_BYTES - 8 * 1024 * 1024))


def _params(estimate_bytes, n_grid):
    return pltpu.CompilerParams(
        dimension_semantics=("arbitrary",) * n_grid,
        vmem_limit_bytes=_vmem_limit(estimate_bytes),
    )


def _layer_norm(y, g, b):
    mu = jnp.mean(y, axis=-1, keepdims=True)
    yc = y - mu
    var = jnp.mean(yc * yc, axis=-1, keepdims=True)
    return yc * lax.rsqrt(var + LN_EPS) * g + b


def _ada_kernel(c_ref, w_ref, b_ref, o_ref):
    c = c_ref[...]
    h = (c * jax.nn.sigmoid(c)).astype(BF16)
    acc = jnp.dot(h, w_ref[0].astype(BF16), preferred_element_type=F32)
    o_ref[0] = acc + b_ref[0]


def _ada(cond, w_ada, b_ada):
    depth, d, n = w_ada.shape
    tn = 1024
    est = 2 * d * tn * 4 + d * tn * 2 + MOD_ROWS * (d + 2 * tn) * 4
    return pl.pallas_call(
        _ada_kernel,
        grid=(depth, n // tn),
        in_specs=[
            pl.BlockSpec((MOD_ROWS, d), lambda i, j: (0, 0)),
            pl.BlockSpec((1, d, tn), lambda i, j: (i, 0, j)),
            pl.BlockSpec((1, 1, tn), lambda i, j: (i, 0, j)),
        ],
        out_specs=pl.BlockSpec((1, MOD_ROWS, tn), lambda i, j: (i, 0, j)),
        out_shape=jax.ShapeDtypeStruct((depth, MOD_ROWS, n), F32),
        compiler_params=_params(est, 2),
        name="ada_mod",
    )(cond, w_ada, b_ada)


def _rope(y, cos, sin_signed, first_half):
    half = HEAD_DIM // 2
    outs = []
    for g in range(y.shape[1] // V7X_LANES):
        yg = y[:, g * V7X_LANES:(g + 1) * V7X_LANES]
        partner = jnp.where(first_half,
                            pltpu.roll(yg, V7X_LANES - half, 1),
                            pltpu.roll(yg, half, 1))
        outs.append(yg * cos + partner * sin_signed)
    return outs


def _proj_kernel(*refs, n_out, rope, scales):
    x_ref, mod_ref = refs[0], refs[1]
    w_refs = refs[2:2 + n_out]
    pos = 2 + n_out
    if rope:
        cos_ref, sin_ref = refs[pos], refs[pos + 1]
        pos += 2
    o_refs = refs[pos:pos + n_out]
    d = x_ref.shape[-1]
    shift = mod_ref[0, :, 0:d]
    scale = mod_ref[0, :, d:2 * d]
    h = (x_ref[0] * (1.0 + scale) + shift).astype(BF16)
    if rope:
        cos = cos_ref[...]
        sin_signed = sin_ref[...]
        lane = lax.broadcasted_iota(jnp.int32, cos.shape, 1)
        first_half = (lane % HEAD_DIM) < (HEAD_DIM // 2)
    for j in range(n_out):
        y = jnp.dot(h, w_refs[j][...], preferred_element_type=F32)
        roped = rope and scales[j] is not None
        if roped:
            parts = _rope(y, cos, sin_signed, first_half)
            for g, part in enumerate(parts):
                if scales[j] != 1.0:
                    part = part * scales[j]
                o_refs[j][0, :, g * V7X_LANES:(g + 1) * V7X_LANES] = part.astype(BF16)
        else:
            if scales[j] is not None and scales[j] != 1.0:
                y = y * scales[j]
            o_refs[j][0] = y.astype(BF16)


def _project(x, mod, weights, rope_tables, scales, tm):
    nb, rows, d = x.shape
    n_out = len(weights)
    rope = rope_tables is not None
    mod_map = (lambda b, i: (b, 0, 0)) if mod.shape[0] == nb and nb > 1 else (lambda b, i: (0, 0, 0))
    in_specs = [
        pl.BlockSpec((1, tm, d), lambda b, i: (b, i, 0)),
        pl.BlockSpec((1, 1, mod.shape[-1]), mod_map),
    ]
    for w in weights:
        in_specs.append(pl.BlockSpec(w.shape, lambda b, i: (0, 0)))
    args = [x, mod] + list(weights)
    if rope:
        in_specs += [pl.BlockSpec((tm, V7X_LANES), lambda b, i: (i, 0))] * 2
        args += list(rope_tables)
    n_tot = sum(w.shape[1] for w in weights)
    est = 2 * tm * d * 4 + 2 * d * n_tot * 2 + 2 * tm * n_tot * 2 + tm * d * 2 + 2 * tm * n_tot * 4
    return pl.pallas_call(
        functools.partial(_proj_kernel, n_out=n_out, rope=rope, scales=tuple(scales)),
        grid=(nb, rows // tm),
        in_specs=in_specs,
        out_specs=[pl.BlockSpec((1, tm, w.shape[1]), lambda b, i: (b, i, 0)) for w in weights],
        out_shape=[jax.ShapeDtypeStruct((nb, rows, w.shape[1]), BF16) for w in weights],
        compiler_params=_params(est, 2),
        name="mod_proj_rope" if rope else "mod_proj",
    )(*args)


def _gqa_block(q, keys, vals, masks, sink_ref, o_ref, row0):
    tq = q.shape[0]
    width = A_KV_HEADS * HEAD_DIM
    contract_last = (((1,), (1,)), ((), ()))
    qs = jnp.concatenate([q[:, h * width:(h + 1) * width] for h in range(A_GROUP)], axis=0)
    q_lane = lax.broadcasted_iota(jnp.int32, qs.shape, 1)
    vals_t = vals.astype(F32).T.astype(BF16)
    masks = [(r0, jnp.concatenate([valid] * A_GROUP, axis=1)) for r0, valid in masks]
    outs = []
    for g in range(A_KV_HEADS):
        qg = jnp.where((q_lane >= g * HEAD_DIM) & (q_lane < (g + 1) * HEAD_DIM), qs, jnp.zeros_like(qs))
        s = lax.dot_general(keys, qg, contract_last, preferred_element_type=F32)
        pieces, row = [], 0
        for r0, valid in masks:
            if r0 > row:
                pieces.append(s[row:r0])
            pieces.append(jnp.where(valid, s[r0:r0 + valid.shape[0]], NEG_INF))
            row = r0 + valid.shape[0]
        if masks:
            s = jnp.concatenate(pieces + [s[row:]], axis=0)
        sink = jnp.concatenate(
            [jnp.full((1, tq), sink_ref[g * A_GROUP + h] * LOG2_E, F32) for h in range(A_GROUP)], axis=1)
        m = jnp.maximum(jnp.max(s, axis=0, keepdims=True), sink)
        p = jnp.exp2(s - m)
        inv = 1.0 / (jnp.sum(p, axis=0, keepdims=True) + jnp.exp2(sink - m))
        acc = jnp.dot(vals_t[g * HEAD_DIM:(g + 1) * HEAD_DIM, :], p.astype(BF16), preferred_element_type=F32)
        outs.append(acc * inv)
    out_t = jnp.concatenate(outs, axis=0)
    for h in range(A_GROUP):
        o_ref[0, row0:row0 + tq, h * width:(h + 1) * width] = out_t[:, h * tq:(h + 1) * tq].T.astype(BF16)


def _window_attn_kernel(sink_ref, q_ref, *refs, q_blocks):
    n_kv = q_blocks + 2
    k_refs, v_refs = refs[:n_kv], refs[n_kv:2 * n_kv]
    kc_ref, vc_ref, o_ref = refs[2 * n_kv:]
    first = pl.program_id(1) * q_blocks
    last = pl.num_programs(1) * q_blocks - 1
    key = lax.broadcasted_iota(jnp.int32, (BLOCK, BLOCK), 0)
    qry = lax.broadcasted_iota(jnp.int32, (BLOCK, BLOCK), 1)
    for t in range(q_blocks):
        masks = [
            (0, (key >= qry) & (first + t > 0)),
            (2 * BLOCK, (key <= qry) & (first + t < last)),
        ]
        keys = jnp.concatenate([k_refs[t][0], k_refs[t + 1][0], k_refs[t + 2][0], kc_ref[0]], axis=0)
        vals = jnp.concatenate([v_refs[t][0], v_refs[t + 1][0], v_refs[t + 2][0], vc_ref[0]], axis=0)
        _gqa_block(q_ref[0, t * BLOCK:(t + 1) * BLOCK, :], keys, vals, masks, sink_ref, o_ref, t * BLOCK)


def _window_attention(q, k, v, kc, vc, sink, q_blocks):
    b, l, dq = q.shape
    dk = k.shape[-1]
    c = kc.shape[1]
    nblk = l // BLOCK
    tq = q_blocks * BLOCK

    def kv_spec(j):
        return pl.BlockSpec((1, BLOCK, dk), lambda bi, n: (bi, jnp.clip(n * q_blocks - 1 + j, 0, nblk - 1), 0))

    kv_specs = [kv_spec(j) for j in range(q_blocks + 2)]
    ctx_spec = pl.BlockSpec((1, c, dk), lambda bi, n: (bi, 0, 0))
    est = 4 * tq * dq * 2 + 8 * (tq + 2 * BLOCK + c) * dk * 2 + 6 * q_blocks * A_GROUP * BLOCK * (3 * BLOCK + c) * 4
    return pl.pallas_call(
        functools.partial(_window_attn_kernel, q_blocks=q_blocks),
        grid=(b, nblk // q_blocks),
        in_specs=[pl.BlockSpec(memory_space=pltpu.SMEM), pl.BlockSpec((1, tq, dq), lambda bi, n: (bi, n, 0))]
        + kv_specs + kv_specs + [ctx_spec, ctx_spec],
        out_specs=pl.BlockSpec((1, tq, dq), lambda bi, n: (bi, n, 0)),
        out_shape=jax.ShapeDtypeStruct((b, l, dq), BF16),
        compiler_params=_params(est, 2),
        name="window_gqa_attn",
    )(sink, q, *([k] * (q_blocks + 2)), *([v] * (q_blocks + 2)), kc, vc)


def _ctx_attn_kernel(sink_ref, q_ref, kc_ref, vc_ref, o_ref):
    _gqa_block(q_ref[0], kc_ref[0], vc_ref[0], [], sink_ref, o_ref, 0)


def _ctx_attention(qc, kc, vc, sink):
    b, c, dq = qc.shape
    dk = kc.shape[-1]
    est = 4 * c * dq * 2 + 4 * c * dk * 2 + 6 * A_GROUP * c * c * 4
    return pl.pallas_call(
        _ctx_attn_kernel,
        grid=(b,),
        in_specs=[
            pl.BlockSpec(memory_space=pltpu.SMEM),
            pl.BlockSpec((1, c, dq), lambda bi: (bi, 0, 0)),
            pl.BlockSpec((1, c, dk), lambda bi: (bi, 0, 0)),
            pl.BlockSpec((1, c, dk), lambda bi: (bi, 0, 0)),
        ],
        out_specs=pl.BlockSpec((1, c, dq), lambda bi: (bi, 0, 0)),
        out_shape=jax.ShapeDtypeStruct((b, c, dq), BF16),
        compiler_params=_params(est, 1),
        name="ctx_gqa_attn",
    )(sink, qc, kc, vc)


def _diff_attn_kernel(lq1_ref, lk1_ref, lq2_ref, lk2_ref, g_ref, q_ref, k_ref, v_ref, kc_ref, vc_ref,
                      o_ref, k_all, v_all_t, *s_bufs, lam_init, tq, heads):
    seq = k_ref.shape[1]
    n_tiles = seq // tq
    hd = 2 * HEAD_DIM
    keys = k_all.shape[1]
    lam = (jnp.exp(jnp.sum(lq1_ref[...] * lk1_ref[...], axis=-1, keepdims=True))
           - jnp.exp(jnp.sum(lq2_ref[...] * lk2_ref[...], axis=-1, keepdims=True)) + lam_init)
    lane = lax.broadcasted_iota(jnp.int32, (tq, hd), 1)
    contract_last = (((1,), (1,)), ((), ()))
    out_gain = g_ref[...] * (1.0 - lam_init)

    pad_row = lax.broadcasted_iota(jnp.int32, (v_all_t.shape[1] - hd, keys), 0)
    for h in range(heads):
        cols = slice(h * hd, (h + 1) * hd)
        k_all[h, :seq, :] = k_ref[0, :, cols]
        k_all[h, seq:, :] = kc_ref[0, :, cols]
        v_all_t[h, :hd, :seq] = v_ref[0, :, cols].astype(F32).T.astype(BF16)
        v_all_t[h, :hd, seq:] = vc_ref[0, :, cols].astype(F32).T.astype(BF16)
        v_all_t[h, hd:, :] = jnp.where(pad_row == 0, 1.0, 0.0).astype(BF16)

    def scores(h, i, s_ref):
        q = q_ref[0, i * tq:(i + 1) * tq, h * hd:(h + 1) * hd]
        zero = jnp.zeros_like(q)
        qq = jnp.concatenate([jnp.where(lane < HEAD_DIM, q, zero), jnp.where(lane >= HEAD_DIM, q, zero)], axis=0)
        s_ref[...] = lax.dot_general(k_all[h], qq, contract_last, preferred_element_type=F32)

    def finish(h, i, s_ref):
        s = s_ref[...]
        p = jnp.exp2(s - jnp.max(s, axis=0, keepdims=True)).astype(BF16)
        acc = jnp.dot(v_all_t[h], p, preferred_element_type=F32)
        n0, n1 = acc[hd:hd + 1, :tq], acc[hd:hd + 1, tq:]
        o_t = acc[:hd, :tq] / n0 - acc[:hd, tq:] * (lam / n1)
        o_t = o_t * lax.rsqrt(jnp.mean(o_t * o_t, axis=0, keepdims=True) + SUBLN_EPS)
        o_ref[0, i * tq:(i + 1) * tq, h * hd:(h + 1) * hd] = (o_t * out_gain).T.astype(BF16)

    work = [(h, i) for i in range(n_tiles) for h in range(heads)]
    scores(*work[0], s_bufs[0])
    for n, (h, i) in enumerate(work):
        if n + 1 < len(work):
            scores(*work[n + 1], s_bufs[(n + 1) % len(s_bufs)])
        finish(h, i, s_bufs[n % len(s_bufs)])


def _diff_attention(q, k, v, kc, vc, lq1, lk1, lq2, lk2, subln_g, lam_init, tq, heads):
    b, l, dm = q.shape
    c = kc.shape[1]
    hd = 2 * HEAD_DIM
    vec = lambda n: pl.BlockSpec((1, n), lambda bi, h: (0, 0))
    per_group = lambda rows: pl.BlockSpec((1, rows, heads * hd), lambda bi, h: (bi, 0, h))
    score_buf = pltpu.VMEM((l + c, 2 * tq), F32)
    est = 2 * heads * (3 * l + 2 * c) * hd * 2 * 2 + 6 * 2 * tq * (l + c) * 4
    return pl.pallas_call(
        functools.partial(_diff_attn_kernel, lam_init=lam_init, tq=tq, heads=heads),
        grid=(b, dm // (heads * hd)),
        in_specs=[
            vec(HEAD_DIM), vec(HEAD_DIM), vec(HEAD_DIM), vec(HEAD_DIM),
            pl.BlockSpec((hd, 1), lambda bi, h: (0, 0)),
            per_group(l), per_group(l), per_group(l), per_group(c), per_group(c),
        ],
        out_specs=per_group(l),
        out_shape=jax.ShapeDtypeStruct((b, l, dm), BF16),
        scratch_shapes=[pltpu.VMEM((heads, l + c, hd), BF16),
                        pltpu.VMEM((heads, hd + BF16_SUBLANES, l + c), BF16), score_buf, score_buf],
        compiler_params=_params(est, 2),
        name="diff_attn",
    )(lq1, lk1, lq2, lk2, subln_g, q, k, v, kc, vc)


def _post_kernel(a_ref, x_ref, mod_ref, wo_ref, w1_ref, w2_ref, ln_ref, o_ref, *, ff_chunk):
    d = x_ref.shape[-1]
    gate1 = mod_ref[0, :, 2 * d:3 * d]
    shift2 = mod_ref[0, :, 3 * d:4 * d]
    scale2 = mod_ref[0, :, 4 * d:5 * d]
    gate2 = mod_ref[0, :, 5 * d:6 * d]
    ax = jnp.dot(a_ref[0], wo_ref[...], preferred_element_type=F32)
    x1 = _layer_norm(DEEPNORM_ALPHA * x_ref[0] + gate1 * ax, ln_ref[0:1, :], ln_ref[1:2, :])
    h = (x1 * (1.0 + scale2) + shift2).astype(BF16)
    fx = jnp.zeros_like(x1)
    for j in range(w1_ref.shape[1] // ff_chunk):
        u = jnp.dot(h, w1_ref[:, j * ff_chunk:(j + 1) * ff_chunk], preferred_element_type=F32)
        u = jnp.maximum(u, 0.0)
        fx = fx + jnp.dot((u * u).astype(BF16), w2_ref[j * ff_chunk:(j + 1) * ff_chunk, :],
                          preferred_element_type=F32)
    o_ref[0] = _layer_norm(DEEPNORM_ALPHA * x1 + gate2 * fx, ln_ref[2:3, :], ln_ref[3:4, :])


def _post(attn, x, mod, wo, w1, w2, ln, tm):
    nb, rows, d = x.shape
    dff = w1.shape[1]
    ff_chunk = 1024
    mod_map = (lambda b, i: (b, 0, 0)) if mod.shape[0] == nb and nb > 1 else (lambda b, i: (0, 0, 0))
    const = lambda shape: pl.BlockSpec(shape, lambda b, i: (0, 0), pipeline_mode=pl.Buffered(1))
    est = ((d * d + 2 * d * dff) * 2 + 2 * tm * d * (2 + 4 + 4)
           + tm * d * 4 * 4 + tm * ff_chunk * 6)
    return pl.pallas_call(
        functools.partial(_post_kernel, ff_chunk=ff_chunk),
        grid=(nb, rows // tm),
        in_specs=[
            pl.BlockSpec((1, tm, d), lambda b, i: (b, i, 0)),
            pl.BlockSpec((1, tm, d), lambda b, i: (b, i, 0)),
            pl.BlockSpec((1, 1, mod.shape[-1]), mod_map),
            const(wo.shape), const(w1.shape), const(w2.shape),
            pl.BlockSpec(ln.shape, lambda b, i: (0, 0)),
        ],
        out_specs=pl.BlockSpec((1, tm, d), lambda b, i: (b, i, 0)),
        out_shape=jax.ShapeDtypeStruct((nb, rows, d), F32),
        compiler_params=_params(est, 2),
        name="oproj_ln_mlp_ln",
    )(attn, x, mod, wo, w1, w2, ln)


def _rope_tables(seq):
    rows = seq // GRID_W
    row = jnp.repeat(jnp.arange(rows, dtype=F32), GRID_W)
    col = jnp.tile(jnp.arange(GRID_W, dtype=F32), rows)
    n_freq = HEAD_DIM // 4
    inv = ROPE_BASE ** (-jnp.arange(n_freq, dtype=F32) / n_freq)
    ang = jnp.concatenate([row[:, None] * inv, col[:, None] * inv], axis=-1)
    cos, sin = jnp.cos(ang), jnp.sin(ang)
    reps = V7X_LANES // HEAD_DIM
    cos_t = jnp.tile(jnp.concatenate([cos, cos], axis=-1), (1, reps))
    sin_t = jnp.tile(jnp.concatenate([-sin, sin], axis=-1), (1, reps))
    return cos_t, sin_t


def kernel(x, c, ctx, c_ctx, w_ada, b_ada, ln1_g, ln1_b, ln2_g, ln2_b, a_wq, a_wk, a_wv, a_wo, a_sink,
           b_wq, b_wk, b_wv, b_wo, b_lq1, b_lk1, b_lq2, b_lk2, b_subln_g, mlp_w1, mlp_w2):
    batch, seq, d = x.shape
    ctx_len = ctx.shape[1]
    tables = _rope_tables(seq)
    bf = lambda w: w.astype(BF16)

    cond = jnp.concatenate([c, c_ctx[None, :], jnp.zeros((MOD_ROWS - batch - 1, d), F32)], axis=0)
    mod = _ada(cond, w_ada, b_ada[:, None, :])

    ctx_rows = 512

    def ctx_project(ctx, mod_c, weights, scales):
        outs = _project(ctx.reshape(1, batch * ctx_len, d), mod_c, weights, None, scales, tm=ctx_rows)
        return [o.reshape(batch, ctx_len, o.shape[-1]) for o in outs]

    for i in range(DEPTH):
        need_ctx = i < DEPTH - 1
        mod_x = mod[i, :batch][:, None, :]
        mod_c = mod[i, batch:batch + 1][:, None, :]
        ln = jnp.stack([ln1_g[i], ln1_b[i], ln2_g[i], ln2_b[i]], axis=0)
        j = i // N_MIXERS
        if i % N_MIXERS == 0:
            wq = a_wq[j].reshape(d, A_KV_HEADS, A_GROUP, HEAD_DIM).swapaxes(1, 2).reshape(d, d)
            wo = a_wo[j].reshape(A_KV_HEADS, A_GROUP, HEAD_DIM, d).swapaxes(0, 1).reshape(d, d)
            wq, wk, wv, wo = bf(wq), bf(a_wk[j]), bf(a_wv[j]), bf(wo)
            q, k, v = _project(x, mod_x, [wq, wk, wv], tables, [QK_SCALE * LOG2_E, 1.0, None], tm=1024)
            if need_ctx:
                qc, kc, vc = ctx_project(ctx, mod_c, [wq, wk, wv], [QK_SCALE * LOG2_E, None, None])
            else:
                kc, vc = ctx_project(ctx, mod_c, [wk, wv], [None, None])
            attn_x = _window_attention(q, k, v, kc, vc, a_sink[j], q_blocks=4)
            if need_ctx:
                attn_c = _ctx_attention(qc, kc, vc, a_sink[j])
        else:
            lam_init = 0.8 - 0.6 * math.exp(-0.3 * i)
            wq, wk, wv, wo = bf(b_wq[j]), bf(b_wk[j]), bf(b_wv[j]), bf(b_wo[j])
            q, k, v = _project(x, mod_x, [wq, wk, wv], tables, [QK_SCALE * LOG2_E, 1.0, None], tm=512)
            if need_ctx:
                raise NotImplementedError("context queries through differential attention")
            kc, vc = ctx_project(ctx, mod_c, [wk, wv], [None, None])
            attn_x = _diff_attention(q, k, v, kc, vc, b_lq1[j][None], b_lk1[j][None], b_lq2[j][None],
                                     b_lk2[j][None], b_subln_g[j][:, None], lam_init, tq=512, heads=2)
        w1, w2 = bf(mlp_w1[i]), bf(mlp_w2[i])
        x = _post(attn_x, x, mod_x, wo, w1, w2, ln, tm=512)
        if need_ctx:
            flat = (1, batch * ctx_len, d)
            ctx = _post(attn_c.reshape(flat), ctx.reshape(flat), mod_c, wo, w1, w2, ln, tm=ctx_rows)
            ctx = ctx.reshape(batch, ctx_len, d)
    return x
```

```python
import functools
import math

import jax
import jax.numpy as jnp
from jax import lax
from jax.experimental import pallas as pl
from jax.experimental.pallas import tpu as pltpu

D_MODEL = 1024
DEPTH = 2
GRID_W = 64
N_MIXERS = 2
BLOCK = 128
HEAD_DIM = 64
A_HEADS = D_MODEL // HEAD_DIM
A_KV_HEADS = max(1, A_HEADS // 8)
A_GROUP = A_HEADS // A_KV_HEADS
B_HEADS = D_MODEL // (2 * HEAD_DIM)
D_FF = 4 * D_MODEL
N_MOD = 6
ROPE_BASE = 10000.0
LN_EPS = 1e-5
SUBLN_EPS = 1e-5
NEG_INF = -1e30
DEEPNORM_ALPHA = (2 * DEPTH) ** 0.25
QK_SCALE = HEAD_DIM ** -0.5
LOG2_E = math.log2(math.e)

V7X_LANES = 128
BF16_SUBLANES = 16
V7X_VMEM_BYTES = 64 * 1024 * 1024
MOD_ROWS = 24

BF16 = jnp.bfloat16
F32 = jnp.float32


def _vmem_limit(estimate_bytes):
    return int(min(max(2 * estimate_bytes, 16 * 1024 * 1024), V7X_VMEM_BYTES - 8 * 1024 * 1024))


def _params(estimate_bytes, n_grid):
    return pltpu.CompilerParams(
        dimension_semantics=("arbitrary",) * n_grid,
        vmem_limit_bytes=_vmem_limit(estimate_bytes),
    )


def _layer_norm(y, g, b):
    mu = jnp.mean(y, axis=-1, keepdims=True)
    yc = y - mu
    var = jnp.mean(yc * yc, axis=-1, keepdims=True)
    return yc * lax.rsqrt(var + LN_EPS) * g + b


def _ada_kernel(c_ref, w_ref, b_ref, o_ref):
    c = c_ref[...]
    h = (c * jax.nn.sigmoid(c)).astype(BF16)
    acc = jnp.dot(h, w_ref[0].astype(BF16), preferred_element_type=F32)
    o_ref[0] = acc + b_ref[0]


def _ada(cond, w_ada, b_ada):
    depth, d, n = w_ada.shape
    tn = 1024
    est = 2 * d * tn * 4 + d * tn * 2 + MOD_ROWS * (d + 2 * tn) * 4
    return pl.pallas_call(
        _ada_kernel,
        grid=(depth, n // tn),
        in_specs=[
            pl.BlockSpec((MOD_ROWS, d), lambda i, j: (0, 0)),
            pl.BlockSpec((1, d, tn), lambda i, j: (i, 0, j)),
            pl.BlockSpec((1, 1, tn), lambda i, j: (i, 0, j)),
        ],
        out_specs=pl.BlockSpec((1, MOD_ROWS, tn), lambda i, j: (i, 0, j)),
        out_shape=jax.ShapeDtypeStruct((depth, MOD_ROWS, n), F32),
        compiler_params=_params(est, 2),
        name="ada_mod",
    )(cond, w_ada, b_ada)


def _rope(y, cos, sin_signed, first_half):
    half = HEAD_DIM // 2
    outs = []
    for g in range(y.shape[1] // V7X_LANES):
        yg = y[:, g * V7X_LANES:(g + 1) * V7X_LANES]
        partner = jnp.where(first_half,
                            pltpu.roll(yg, V7X_LANES - half, 1),
                            pltpu.roll(yg, half, 1))
        outs.append(yg * cos + partner * sin_signed)
    return outs


def _proj_kernel(*refs, n_out, rope, scales):
    x_ref, mod_ref = refs[0], refs[1]
    w_refs = refs[2:2 + n_out]
    pos = 2 + n_out
    if rope:
        cos_ref, sin_ref = refs[pos], refs[pos + 1]
        pos += 2
    o_refs = refs[pos:pos + n_out]
    d = x_ref.shape[-1]
    shift = mod_ref[0, :, 0:d]
    scale = mod_ref[0, :, d:2 * d]
    h = (x_ref[0] * (1.0 + scale) + shift).astype(BF16)
    if rope:
        cos = cos_ref[...]
        sin_signed = sin_ref[...]
        lane = lax.broadcasted_iota(jnp.int32, cos.shape, 1)
        first_half = (lane % HEAD_DIM) < (HEAD_DIM // 2)
    for j in range(n_out):
        y = jnp.dot(h, w_refs[j][...], preferred_element_type=F32)
        roped = rope and scales[j] is not None
        if roped:
            parts = _rope(y, cos, sin_signed, first_half)
            for g, part in enumerate(parts):
                if scales[j] != 1.0:
                    part = part * scales[j]
                o_refs[j][0, :, g * V7X_LANES:(g + 1) * V7X_LANES] = part.astype(BF16)
        else:
            if scales[j] is not None and scales[j] != 1.0:
                y = y * scales[j]
            o_refs[j][0] = y.astype(BF16)


def _project(x, mod, weights, rope_tables, scales, tm):
    nb, rows, d = x.shape
    n_out = len(weights)
    rope = rope_tables is not None
    mod_map = (lambda b, i: (b, 0, 0)) if mod.shape[0] == nb and nb > 1 else (lambda b, i: (0, 0, 0))
    in_specs = [
        pl.BlockSpec((1, tm, d), lambda b, i: (b, i, 0)),
        pl.BlockSpec((1, 1, mod.shape[-1]), mod_map),
    ]
    for w in weights:
        in_specs.append(pl.BlockSpec(w.shape, lambda b, i: (0, 0)))
    args = [x, mod] + list(weights)
    if rope:
        in_specs += [pl.BlockSpec((tm, V7X_LANES), lambda b, i: (i, 0))] * 2
        args += list(rope_tables)
    n_tot = sum(w.shape[1] for w in weights)
    est = 2 * tm * d * 4 + 2 * d * n_tot * 2 + 2 * tm * n_tot * 2 + tm * d * 2 + 2 * tm * n_tot * 4
    return pl.pallas_call(
        functools.partial(_proj_kernel, n_out=n_out, rope=rope, scales=tuple(scales)),
        grid=(nb, rows // tm),
        in_specs=in_specs,
        out_specs=[pl.BlockSpec((1, tm, w.shape[1]), lambda b, i: (b, i, 0)) for w in weights],
        out_shape=[jax.ShapeDtypeStruct((nb, rows, w.shape[1]), BF16) for w in weights],
        compiler_params=_params(est, 2),
        name="mod_proj_rope" if rope else "mod_proj",
    )(*args)


def _gqa_block(q, keys, vals, masks, sink_ref, o_ref, row0):
    tq = q.shape[0]
    width = A_KV_HEADS * HEAD_DIM
    contract_last = (((1,), (1,)), ((), ()))
    qs = jnp.concatenate([q[:, h * width:(h + 1) * width] for h in range(A_GROUP)], axis=0)
    q_lane = lax.broadcasted_iota(jnp.int32, qs.shape, 1)
    vals_t = vals.astype(F32).T.astype(BF16)
    masks = [(r0, jnp.concatenate([valid] * A_GROUP, axis=1)) for r0, valid in masks]
    outs = []
    for g in range(A_KV_HEADS):
        qg = jnp.where((q_lane >= g * HEAD_DIM) & (q_lane < (g + 1) * HEAD_DIM), qs, jnp.zeros_like(qs))
        s = lax.dot_general(keys, qg, contract_last, preferred_element_type=F32)
        pieces, row = [], 0
        for r0, valid in masks:
            if r0 > row:
                pieces.append(s[row:r0])
            pieces.append(jnp.where(valid, s[r0:r0 + valid.shape[0]], NEG_INF))
            row = r0 + valid.shape[0]
        if masks:
            s = jnp.concatenate(pieces + [s[row:]], axis=0)
        sink = jnp.concatenate(
            [jnp.full((1, tq), sink_ref[g * A_GROUP + h] * LOG2_E, F32) for h in range(A_GROUP)], axis=1)
        m = jnp.maximum(jnp.max(s, axis=0, keepdims=True), sink)
        p = jnp.exp2(s - m)
        inv = 1.0 / (jnp.sum(p, axis=0, keepdims=True) + jnp.exp2(sink - m))
        acc = jnp.dot(vals_t[g * HEAD_DIM:(g + 1) * HEAD_DIM, :], p.astype(BF16), preferred_element_type=F32)
        outs.append(acc * inv)
    out_t = jnp.concatenate(outs, axis=0)
    for h in range(A_GROUP):
        o_ref[0, row0:row0 + tq, h * width:(h + 1) * width] = out_t[:, h * tq:(h + 1) * tq].T.astype(BF16)


def _window_attn_kernel(sink_ref, q_ref, *refs, q_blocks):
    n_kv = q_blocks + 2
    k_refs, v_refs = refs[:n_kv], refs[n_kv:2 * n_kv]
    kc_ref, vc_ref, o_ref = refs[2 * n_kv:]
    first = pl.program_id(1) * q_blocks
    last = pl.num_programs(1) * q_blocks - 1
    key = lax.broadcasted_iota(jnp.int32, (BLOCK, BLOCK), 0)
    qry = lax.broadcasted_iota(jnp.int32, (BLOCK, BLOCK), 1)
    for t in range(q_blocks):
        masks = [
            (0, (key >= qry) & (first + t > 0)),
            (2 * BLOCK, (key <= qry) & (first + t < last)),
        ]
        keys = jnp.concatenate([k_refs[t][0], k_refs[t + 1][0], k_refs[t + 2][0], kc_ref[0]], axis=0)
        vals = jnp.concatenate([v_refs[t][0], v_refs[t + 1][0], v_refs[t + 2][0], vc_ref[0]], axis=0)
        _gqa_block(q_ref[0, t * BLOCK:(t + 1) * BLOCK, :], keys, vals, masks, sink_ref, o_ref, t * BLOCK)


def _window_attention(q, k, v, kc, vc, sink, q_blocks):
    b, l, dq = q.shape
    dk = k.shape[-1]
    c = kc.shape[1]
    nblk = l // BLOCK
    tq = q_blocks * BLOCK

    def kv_spec(j):
        return pl.BlockSpec((1, BLOCK, dk), lambda bi, n: (bi, jnp.clip(n * q_blocks - 1 + j, 0, nblk - 1), 0))

    kv_specs = [kv_spec(j) for j in range(q_blocks + 2)]
    ctx_spec = pl.BlockSpec((1, c, dk), lambda bi, n: (bi, 0, 0))
    est = 4 * tq * dq * 2 + 8 * (tq + 2 * BLOCK + c) * dk * 2 + 6 * q_blocks * A_GROUP * BLOCK * (3 * BLOCK + c) * 4
    return pl.pallas_call(
        functools.partial(_window_attn_kernel, q_blocks=q_blocks),
        grid=(b, nblk // q_blocks),
        in_specs=[pl.BlockSpec(memory_space=pltpu.SMEM), pl.BlockSpec((1, tq, dq), lambda bi, n: (bi, n, 0))]
        + kv_specs + kv_specs + [ctx_spec, ctx_spec],
        out_specs=pl.BlockSpec((1, tq, dq), lambda bi, n: (bi, n, 0)),
        out_shape=jax.ShapeDtypeStruct((b, l, dq), BF16),
        compiler_params=_params(est, 2),
        name="window_gqa_attn",
    )(sink, q, *([k] * (q_blocks + 2)), *([v] * (q_blocks + 2)), kc, vc)


def _ctx_attn_kernel(sink_ref, q_ref, kc_ref, vc_ref, o_ref):
    _gqa_block(q_ref[0], kc_ref[0], vc_ref[0], [], sink_ref, o_ref, 0)


def _ctx_attention(qc, kc, vc, sink):
    b, c, dq = qc.shape
    dk = kc.shape[-1]
    est = 4 * c * dq * 2 + 4 * c * dk * 2 + 6 * A_GROUP * c * c * 4
    return pl.pallas_call(
        _ctx_attn_kernel,
        grid=(b,),
        in_specs=[
            pl.BlockSpec(memory_space=pltpu.SMEM),
            pl.BlockSpec((1, c, dq), lambda bi: (bi, 0, 0)),
            pl.BlockSpec((1, c, dk), lambda bi: (bi, 0, 0)),
            pl.BlockSpec((1, c, dk), lambda bi: (bi, 0, 0)),
        ],
        out_specs=pl.BlockSpec((1, c, dq), lambda bi: (bi, 0, 0)),
        out_shape=jax.ShapeDtypeStruct((b, c, dq), BF16),
        compiler_params=_params(est, 1),
        name="ctx_gqa_attn",
    )(sink, qc, kc, vc)


def _diff_attn_kernel(lq1_ref, lk1_ref, lq2_ref, lk2_ref, g_ref, q_ref, k_ref, v_ref, kc_ref, vc_ref,
                      o_ref, k_all, v_all_t, *s_bufs, lam_init, tq, heads):
    seq = k_ref.shape[1]
    n_tiles = seq // tq
    hd = 2 * HEAD_DIM
    keys = k_all.shape[1]
    lam = (jnp.exp(jnp.sum(lq1_ref[...] * lk1_ref[...], axis=-1, keepdims=True))
           - jnp.exp(jnp.sum(lq2_ref[...] * lk2_ref[...], axis=-1, keepdims=True)) + lam_init)
    lane = lax.broadcasted_iota(jnp.int32, (tq, hd), 1)
    contract_last = (((1,), (1,)), ((), ()))
    out_gain = g_ref[...] * (1.0 - lam_init)

    pad_row = lax.broadcasted_iota(jnp.int32, (v_all_t.shape[1] - hd, keys), 0)
    for h in range(heads):
        cols = slice(h * hd, (h + 1) * hd)
        k_all[h, :seq, :] = k_ref[0, :, cols]
        k_all[h, seq:, :] = kc_ref[0, :, cols]
        v_all_t[h, :hd, :seq] = v_ref[0, :, cols].astype(F32).T.astype(BF16)
        v_all_t[h, :hd, seq:] = vc_ref[0, :, cols].astype(F32).T.astype(BF16)
        v_all_t[h, hd:, :] = jnp.where(pad_row == 0, 1.0, 0.0).astype(BF16)

    def scores(h, i, s_ref):
        q = q_ref[0, i * tq:(i + 1) * tq, h * hd:(h + 1) * hd]
        zero = jnp.zeros_like(q)
        qq = jnp.concatenate([jnp.where(lane < HEAD_DIM, q, zero), jnp.where(lane >= HEAD_DIM, q, zero)], axis=0)
        s_ref[...] = lax.dot_general(k_all[h], qq, contract_last, preferred_element_type=F32)

    def finish(h, i, s_ref):
        s = s_ref[...]
        p = jnp.exp2(s - jnp.max(s, axis=0, keepdims=True)).astype(BF16)
        acc = jnp.dot(v_all_t[h], p, preferred_element_type=F32)
        n0, n1 = acc[hd:hd + 1, :tq], acc[hd:hd + 1, tq:]
        o_t = acc[:hd, :tq] / n0 - acc[:hd, tq:] * (lam / n1)
        o_t = o_t * lax.rsqrt(jnp.mean(o_t * o_t, axis=0, keepdims=True) + SUBLN_EPS)
        o_ref[0, i * tq:(i + 1) * tq, h * hd:(h + 1) * hd] = (o_t * out_gain).T.astype(BF16)

    work = [(h, i) for i in range(n_tiles) for h in range(heads)]
    scores(*work[0], s_bufs[0])
    for n, (h, i) in enumerate(work):
        if n + 1 < len(work):
            scores(*work[n + 1], s_bufs[(n + 1) % len(s_bufs)])
        finish(h, i, s_bufs[n % len(s_bufs)])


def _diff_attention(q, k, v, kc, vc, lq1, lk1, lq2, lk2, subln_g, lam_init, tq, heads):
    b, l, dm = q.shape
    c = kc.shape[1]
    hd = 2 * HEAD_DIM
    vec = lambda n: pl.BlockSpec((1, n), lambda bi, h: (0, 0))
    per_group = lambda rows: pl.BlockSpec((1, rows, heads * hd), lambda bi, h: (bi, 0, h))
    score_buf = pltpu.VMEM((l + c, 2 * tq), F32)
    est = 2 * heads * (3 * l + 2 * c) * hd * 2 * 2 + 6 * 2 * tq * (l + c) * 4
    return pl.pallas_call(
        functools.partial(_diff_attn_kernel, lam_init=lam_init, tq=tq, heads=heads),
        grid=(b, dm // (heads * hd)),
        in_specs=[
            vec(HEAD_DIM), vec(HEAD_DIM), vec(HEAD_DIM), vec(HEAD_DIM),
            pl.BlockSpec((hd, 1), lambda bi, h: (0, 0)),
            per_group(l), per_group(l), per_group(l), per_group(c), per_group(c),
        ],
        out_specs=per_group(l),
        out_shape=jax.ShapeDtypeStruct((b, l, dm), BF16),
        scratch_shapes=[pltpu.VMEM((heads, l + c, hd), BF16),
                        pltpu.VMEM((heads, hd + BF16_SUBLANES, l + c), BF16), score_buf, score_buf],
        compiler_params=_params(est, 2),
        name="diff_attn",
    )(lq1, lk1, lq2, lk2, subln_g, q, k, v, kc, vc)


def _post_kernel(a_ref, x_ref, mod_ref, wo_ref, w1_ref, w2_ref, ln_ref, o_ref, *, ff_chunk):
    d = x_ref.shape[-1]
    gate1 = mod_ref[0, :, 2 * d:3 * d]
    shift2 = mod_ref[0, :, 3 * d:4 * d]
    scale2 = mod_ref[0, :, 4 * d:5 * d]
    gate2 = mod_ref[0, :, 5 * d:6 * d]
    ax = jnp.dot(a_ref[0], wo_ref[...], preferred_element_type=F32)
    x1 = _layer_norm(DEEPNORM_ALPHA * x_ref[0] + gate1 * ax, ln_ref[0:1, :], ln_ref[1:2, :])
    h = (x1 * (1.0 + scale2) + shift2).astype(BF16)
    fx = jnp.zeros_like(x1)
    for j in range(w1_ref.shape[2] // ff_chunk):
        u = jnp.dot(h, w1_ref[0, :, j * ff_chunk:(j + 1) * ff_chunk], preferred_element_type=F32)
        u = jnp.maximum(u, 0.0)
        fx = fx + jnp.dot((u * u).astype(BF16), w2_ref[0, j * ff_chunk:(j + 1) * ff_chunk, :],
                          preferred_element_type=F32)
    o_ref[0] = _layer_norm(DEEPNORM_ALPHA * x1 + gate2 * fx, ln_ref[2:3, :], ln_ref[3:4, :])


def _post(attn, x, mod, wo, w1, w2, ln, tm, layer):
    nb, rows, d = x.shape
    dff = w1.shape[2]
    ff_chunk = 1024
    mod_map = (lambda b, i: (b, 0, 0)) if mod.shape[0] == nb and nb > 1 else (lambda b, i: (0, 0, 0))
    const = lambda shape: pl.BlockSpec(shape, lambda b, i: (0, 0), pipeline_mode=pl.Buffered(1))
    of_layer = lambda w: pl.BlockSpec((1,) + w.shape[1:], lambda b, i: (layer, 0, 0), pipeline_mode=pl.Buffered(1))
    est = ((d * d + 2 * d * dff) * 2 + 2 * tm * d * (2 + 4 + 4)
           + tm * d * 4 * 4 + tm * ff_chunk * 6)
    return pl.pallas_call(
        functools.partial(_post_kernel, ff_chunk=ff_chunk),
        grid=(nb, rows // tm),
        in_specs=[
            pl.BlockSpec((1, tm, d), lambda b, i: (b, i, 0)),
            pl.BlockSpec((1, tm, d), lambda b, i: (b, i, 0)),
            pl.BlockSpec((1, 1, mod.shape[-1]), mod_map),
            const(wo.shape), of_layer(w1), of_layer(w2),
            pl.BlockSpec(ln.shape, lambda b, i: (0, 0)),
        ],
        out_specs=pl.BlockSpec((1, tm, d), lambda b, i: (b, i, 0)),
        out_shape=jax.ShapeDtypeStruct((nb, rows, d), F32),
        compiler_params=_params(est, 2),
        name="oproj_ln_mlp_ln",
    )(attn, x, mod, wo, w1, w2, ln)


def _rope_tables(seq):
    rows = seq // GRID_W
    row = jnp.repeat(jnp.arange(rows, dtype=F32), GRID_W)
    col = jnp.tile(jnp.arange(GRID_W, dtype=F32), rows)
    n_freq = HEAD_DIM // 4
    inv = ROPE_BASE ** (-jnp.arange(n_freq, dtype=F32) / n_freq)
    ang = jnp.concatenate([row[:, None] * inv, col[:, None] * inv], axis=-1)
    cos, sin = jnp.cos(ang), jnp.sin(ang)
    reps = V7X_LANES // HEAD_DIM
    cos_t = jnp.tile(jnp.concatenate([cos, cos], axis=-1), (1, reps))
    sin_t = jnp.tile(jnp.concatenate([-sin, sin], axis=-1), (1, reps))
    return cos_t, sin_t


def kernel(x, c, ctx, c_ctx, w_ada, b_ada, ln1_g, ln1_b, ln2_g, ln2_b, a_wq, a_wk, a_wv, a_wo, a_sink,
           b_wq, b_wk, b_wv, b_wo, b_lq1, b_lk1, b_lq2, b_lk2, b_subln_g, mlp_w1, mlp_w2):
    batch, seq, d = x.shape
    ctx_len = ctx.shape[1]
    tables = _rope_tables(seq)
    bf = lambda w: w.astype(BF16)

    cond = jnp.concatenate([c, c_ctx[None, :], jnp.zeros((MOD_ROWS - batch - 1, d), F32)], axis=0)
    mod = _ada(cond, w_ada, b_ada[:, None, :])

    w1, w2 = bf(mlp_w1), bf(mlp_w2)
    ctx_rows = 512

    def ctx_project(ctx, mod_c, weights, scales):
        outs = _project(ctx.reshape(1, batch * ctx_len, d), mod_c, weights, None, scales, tm=ctx_rows)
        return [o.reshape(batch, ctx_len, o.shape[-1]) for o in outs]

    for i in range(DEPTH):
        need_ctx = i < DEPTH - 1
        mod_x = mod[i, :batch][:, None, :]
        mod_c = mod[i, batch:batch + 1][:, None, :]
        ln = jnp.stack([ln1_g[i], ln1_b[i], ln2_g[i], ln2_b[i]], axis=0)
        j = i // N_MIXERS
        if i % N_MIXERS == 0:
            wq = a_wq[j].reshape(d, A_KV_HEADS, A_GROUP, HEAD_DIM).swapaxes(1, 2).reshape(d, d)
            wo = a_wo[j].reshape(A_KV_HEADS, A_GROUP, HEAD_DIM, d).swapaxes(0, 1).reshape(d, d)
            wq, wk, wv, wo = bf(wq), bf(a_wk[j]), bf(a_wv[j]), bf(wo)
            q, k, v = _project(x, mod_x, [wq, wk, wv], tables, [QK_SCALE * LOG2_E, 1.0, None], tm=1024)
            if need_ctx:
                qc, kc, vc = ctx_project(ctx, mod_c, [wq, wk, wv], [QK_SCALE * LOG2_E, None, None])
            else:
                kc, vc = ctx_project(ctx, mod_c, [wk, wv], [None, None])
            attn_x = _window_attention(q, k, v, kc, vc, a_sink[j], q_blocks=4)
            if need_ctx:
                attn_c = _ctx_attention(qc, kc, vc, a_sink[j])
        else:
            lam_init = 0.8 - 0.6 * math.exp(-0.3 * i)
            wq, wk, wv, wo = bf(b_wq[j]), bf(b_wk[j]), bf(b_wv[j]), bf(b_wo[j])
            q, k, v = _project(x, mod_x, [wq, wk, wv], tables, [QK_SCALE * LOG2_E, 1.0, None], tm=512)
            if need_ctx:
                raise NotImplementedError("context queries through differential attention")
            kc, vc = ctx_project(ctx, mod_c, [wk, wv], [None, None])
            attn_x = _diff_attention(q, k, v, kc, vc, b_lq1[j][None], b_lk1[j][None], b_lq2[j][None],
                                     b_lk2[j][None], b_subln_g[j][:, None], lam_init, tq=512, heads=2)
        x = _post(attn_x, x, mod_x, wo, w1, w2, ln, tm=512, layer=i)
        if need_ctx:
            flat = (1, batch * ctx_len, d)
            ctx = _post(attn_c.reshape(flat), ctx.reshape(flat), mod_c, wo, w1, w2, ln, tm=ctx_rows, layer=i)
            ctx = ctx.reshape(batch, ctx_len, d)
    return x
```

```python
import functools
import math

import jax
import jax.numpy as jnp
from jax import lax
from jax.experimental import pallas as pl
from jax.experimental.pallas import tpu as pltpu

D_MODEL = 1024
DEPTH = 2
GRID_W = 64
N_MIXERS = 2
BLOCK = 128
HEAD_DIM = 64
A_HEADS = D_MODEL // HEAD_DIM
A_KV_HEADS = max(1, A_HEADS // 8)
A_GROUP = A_HEADS // A_KV_HEADS
B_HEADS = D_MODEL // (2 * HEAD_DIM)
D_FF = 4 * D_MODEL
N_MOD = 6
ROPE_BASE = 10000.0
LN_EPS = 1e-5
SUBLN_EPS = 1e-5
NEG_INF = -1e30
DEEPNORM_ALPHA = (2 * DEPTH) ** 0.25
QK_SCALE = HEAD_DIM ** -0.5
LOG2_E = math.log2(math.e)

V7X_LANES = 128
BF16_SUBLANES = 16
V7X_VMEM_BYTES = 64 * 1024 * 1024
MOD_ROWS = 24

BF16 = jnp.bfloat16
F32 = jnp.float32


def _vmem_limit(estimate_bytes):
    return int(min(max(2 * estimate_bytes, 16 * 1024 * 1024), V7X_VMEM_BYTES - 8 * 1024 * 1024))


def _params(estimate_bytes, n_grid):
    return pltpu.CompilerParams(
        dimension_semantics=("arbitrary",) * n_grid,
        vmem_limit_bytes=_vmem_limit(estimate_bytes),
    )


def _layer_norm(y, g, b):
    mu = jnp.mean(y, axis=-1, keepdims=True)
    yc = y - mu
    var = jnp.mean(yc * yc, axis=-1, keepdims=True)
    return yc * lax.rsqrt(var + LN_EPS) * g + b


def _ada_kernel(c_ref, w_ref, b_ref, o_ref):
    c = c_ref[...]
    h = (c * jax.nn.sigmoid(c)).astype(BF16)
    acc = jnp.dot(h, w_ref[0].astype(BF16), preferred_element_type=F32)
    o_ref[0] = acc + b_ref[0]


def _ada(cond, w_ada, b_ada):
    depth, d, n = w_ada.shape
    tn = 1024
    est = 2 * d * tn * 4 + d * tn * 2 + MOD_ROWS * (d + 2 * tn) * 4
    return pl.pallas_call(
        _ada_kernel,
        grid=(depth, n // tn),
        in_specs=[
            pl.BlockSpec((MOD_ROWS, d), lambda i, j: (0, 0)),
            pl.BlockSpec((1, d, tn), lambda i, j: (i, 0, j)),
            pl.BlockSpec((1, 1, tn), lambda i, j: (i, 0, j)),
        ],
        out_specs=pl.BlockSpec((1, MOD_ROWS, tn), lambda i, j: (i, 0, j)),
        out_shape=jax.ShapeDtypeStruct((depth, MOD_ROWS, n), F32),
        compiler_params=_params(est, 2),
        name="ada_mod",
    )(cond, w_ada, b_ada)


def _rope(y, cos, sin_signed, first_half):
    half = HEAD_DIM // 2
    outs = []
    for g in range(y.shape[1] // V7X_LANES):
        yg = y[:, g * V7X_LANES:(g + 1) * V7X_LANES]
        partner = jnp.where(first_half,
                            pltpu.roll(yg, V7X_LANES - half, 1),
                            pltpu.roll(yg, half, 1))
        outs.append(yg * cos + partner * sin_signed)
    return outs


def _proj_kernel(*refs, n_out, rope, scales):
    x_ref, mod_ref = refs[0], refs[1]
    w_refs = refs[2:2 + n_out]
    pos = 2 + n_out
    if rope:
        cos_ref, sin_ref = refs[pos], refs[pos + 1]
        pos += 2
    o_refs = refs[pos:pos + n_out]
    d = x_ref.shape[-1]
    shift = mod_ref[0, :, 0:d]
    scale = mod_ref[0, :, d:2 * d]
    h = (x_ref[0] * (1.0 + scale) + shift).astype(BF16)
    if rope:
        cos = cos_ref[...]
        sin_signed = sin_ref[...]
        lane = lax.broadcasted_iota(jnp.int32, cos.shape, 1)
        first_half = (lane % HEAD_DIM) < (HEAD_DIM // 2)
    for j in range(n_out):
        y = jnp.dot(h, w_refs[j][...], preferred_element_type=F32)
        roped = rope and scales[j] is not None
        if roped:
            parts = _rope(y, cos, sin_signed, first_half)
            for g, part in enumerate(parts):
                if scales[j] != 1.0:
                    part = part * scales[j]
                o_refs[j][0, :, g * V7X_LANES:(g + 1) * V7X_LANES] = part.astype(BF16)
        else:
            if scales[j] is not None and scales[j] != 1.0:
                y = y * scales[j]
            o_refs[j][0] = y.astype(BF16)


def _project(x, mod, weights, rope_tables, scales, tm):
    nb, rows, d = x.shape
    n_out = len(weights)
    rope = rope_tables is not None
    mod_map = (lambda b, i: (b, 0, 0)) if mod.shape[0] == nb and nb > 1 else (lambda b, i: (0, 0, 0))
    in_specs = [
        pl.BlockSpec((1, tm, d), lambda b, i: (b, i, 0)),
        pl.BlockSpec((1, 1, mod.shape[-1]), mod_map),
    ]
    for w in weights:
        in_specs.append(pl.BlockSpec(w.shape, lambda b, i: (0, 0)))
    args = [x, mod] + list(weights)
    if rope:
        in_specs += [pl.BlockSpec((tm, V7X_LANES), lambda b, i: (i, 0))] * 2
        args += list(rope_tables)
    n_tot = sum(w.shape[1] for w in weights)
    est = 2 * tm * d * 4 + 2 * d * n_tot * 2 + 2 * tm * n_tot * 2 + tm * d * 2 + 2 * tm * n_tot * 4
    return pl.pallas_call(
        functools.partial(_proj_kernel, n_out=n_out, rope=rope, scales=tuple(scales)),
        grid=(nb, rows // tm),
        in_specs=in_specs,
        out_specs=[pl.BlockSpec((1, tm, w.shape[1]), lambda b, i: (b, i, 0)) for w in weights],
        out_shape=[jax.ShapeDtypeStruct((nb, rows, w.shape[1]), BF16) for w in weights],
        compiler_params=_params(est, 2),
        name="mod_proj_rope" if rope else "mod_proj",
    )(*args)


def _gqa_block(q, keys, vals, masks, sink_ref, o_ref, row0):
    tq = q.shape[0]
    width = A_KV_HEADS * HEAD_DIM
    contract_last = (((1,), (1,)), ((), ()))
    qs = jnp.concatenate([q[:, h * width:(h + 1) * width] for h in range(A_GROUP)], axis=0)
    q_lane = lax.broadcasted_iota(jnp.int32, qs.shape, 1)
    vals_t = vals.astype(F32).T.astype(BF16)
    masks = [(r0, jnp.concatenate([valid] * A_GROUP, axis=1)) for r0, valid in masks]
    outs = []
    for g in range(A_KV_HEADS):
        qg = jnp.where((q_lane >= g * HEAD_DIM) & (q_lane < (g + 1) * HEAD_DIM), qs, jnp.zeros_like(qs))
        s = lax.dot_general(keys, qg, contract_last, preferred_element_type=F32)
        pieces, row = [], 0
        for r0, valid in masks:
            if r0 > row:
                pieces.append(s[row:r0])
            pieces.append(jnp.where(valid, s[r0:r0 + valid.shape[0]], NEG_INF))
            row = r0 + valid.shape[0]
        if masks:
            s = jnp.concatenate(pieces + [s[row:]], axis=0)
        sink = jnp.concatenate(
            [jnp.full((1, tq), sink_ref[g * A_GROUP + h] * LOG2_E, F32) for h in range(A_GROUP)], axis=1)
        m = jnp.maximum(jnp.max(s, axis=0, keepdims=True), sink)
        p = jnp.exp2(s - m)
        inv = 1.0 / (jnp.sum(p, axis=0, keepdims=True) + jnp.exp2(sink - m))
        acc = jnp.dot(vals_t[g * HEAD_DIM:(g + 1) * HEAD_DIM, :], p.astype(BF16), preferred_element_type=F32)
        outs.append(acc * inv)
    out_t = jnp.concatenate(outs, axis=0)
    for h in range(A_GROUP):
        o_ref[0, row0:row0 + tq, h * width:(h + 1) * width] = out_t[:, h * tq:(h + 1) * tq].T.astype(BF16)


def _window_attn_kernel(sink_ref, q_ref, *refs, q_blocks):
    n_kv = q_blocks + 2
    k_refs, v_refs = refs[:n_kv], refs[n_kv:2 * n_kv]
    kc_ref, vc_ref, o_ref = refs[2 * n_kv:]
    first = pl.program_id(1) * q_blocks
    last = pl.num_programs(1) * q_blocks - 1
    key = lax.broadcasted_iota(jnp.int32, (BLOCK, BLOCK), 0)
    qry = lax.broadcasted_iota(jnp.int32, (BLOCK, BLOCK), 1)
    for t in range(q_blocks):
        masks = [
            (0, (key >= qry) & (first + t > 0)),
            (2 * BLOCK, (key <= qry) & (first + t < last)),
        ]
        keys = jnp.concatenate([k_refs[t][0], k_refs[t + 1][0], k_refs[t + 2][0], kc_ref[0]], axis=0)
        vals = jnp.concatenate([v_refs[t][0], v_refs[t + 1][0], v_refs[t + 2][0], vc_ref[0]], axis=0)
        _gqa_block(q_ref[0, t * BLOCK:(t + 1) * BLOCK, :], keys, vals, masks, sink_ref, o_ref, t * BLOCK)


def _window_attention(q, k, v, kc, vc, sink, q_blocks):
    b, l, dq = q.shape
    dk = k.shape[-1]
    c = kc.shape[1]
    nblk = l // BLOCK
    tq = q_blocks * BLOCK

    def kv_spec(j):
        return pl.BlockSpec((1, BLOCK, dk), lambda bi, n: (bi, jnp.clip(n * q_blocks - 1 + j, 0, nblk - 1), 0))

    kv_specs = [kv_spec(j) for j in range(q_blocks + 2)]
    ctx_spec = pl.BlockSpec((1, c, dk), lambda bi, n: (bi, 0, 0))
    est = 4 * tq * dq * 2 + 8 * (tq + 2 * BLOCK + c) * dk * 2 + 6 * q_blocks * A_GROUP * BLOCK * (3 * BLOCK + c) * 4
    return pl.pallas_call(
        functools.partial(_window_attn_kernel, q_blocks=q_blocks),
        grid=(b, nblk // q_blocks),
        in_specs=[pl.BlockSpec(memory_space=pltpu.SMEM), pl.BlockSpec((1, tq, dq), lambda bi, n: (bi, n, 0))]
        + kv_specs + kv_specs + [ctx_spec, ctx_spec],
        out_specs=pl.BlockSpec((1, tq, dq), lambda bi, n: (bi, n, 0)),
        out_shape=jax.ShapeDtypeStruct((b, l, dq), BF16),
        compiler_params=_params(est, 2),
        name="window_gqa_attn",
    )(sink, q, *([k] * (q_blocks + 2)), *([v] * (q_blocks + 2)), kc, vc)


def _ctx_attn_kernel(sink_ref, q_ref, kc_ref, vc_ref, o_ref):
    _gqa_block(q_ref[0], kc_ref[0], vc_ref[0], [], sink_ref, o_ref, 0)


def _ctx_attention(qc, kc, vc, sink):
    b, c, dq = qc.shape
    dk = kc.shape[-1]
    est = 4 * c * dq * 2 + 4 * c * dk * 2 + 6 * A_GROUP * c * c * 4
    return pl.pallas_call(
        _ctx_attn_kernel,
        grid=(b,),
        in_specs=[
            pl.BlockSpec(memory_space=pltpu.SMEM),
            pl.BlockSpec((1, c, dq), lambda bi: (bi, 0, 0)),
            pl.BlockSpec((1, c, dk), lambda bi: (bi, 0, 0)),
            pl.BlockSpec((1, c, dk), lambda bi: (bi, 0, 0)),
        ],
        out_specs=pl.BlockSpec((1, c, dq), lambda bi: (bi, 0, 0)),
        out_shape=jax.ShapeDtypeStruct((b, c, dq), BF16),
        compiler_params=_params(est, 1),
        name="ctx_gqa_attn",
    )(sink, qc, kc, vc)


def _diff_attn_kernel(lq1_ref, lk1_ref, lq2_ref, lk2_ref, g_ref, q_ref, k_ref, v_ref, kc_ref, vc_ref,
                      o_ref, k_all, v_all_t, *s_bufs, lam_init, tq, heads):
    seq = k_ref.shape[1]
    n_tiles = seq // tq
    hd = 2 * HEAD_DIM
    keys = k_all.shape[1]
    lam = (jnp.exp(jnp.sum(lq1_ref[...] * lk1_ref[...], axis=-1, keepdims=True))
           - jnp.exp(jnp.sum(lq2_ref[...] * lk2_ref[...], axis=-1, keepdims=True)) + lam_init)
    lane = lax.broadcasted_iota(jnp.int32, (tq, hd), 1)
    contract_last = (((1,), (1,)), ((), ()))
    out_gain = g_ref[...] * (1.0 - lam_init)

    pad_row = lax.broadcasted_iota(jnp.int32, (v_all_t.shape[1] - hd, keys), 0)
    for h in range(heads):
        cols = slice(h * hd, (h + 1) * hd)
        k_all[h, :seq, :] = k_ref[0, :, cols]
        k_all[h, seq:, :] = kc_ref[0, :, cols]
        v_all_t[h, :hd, :seq] = v_ref[0, :, cols].astype(F32).T.astype(BF16)
        v_all_t[h, :hd, seq:] = vc_ref[0, :, cols].astype(F32).T.astype(BF16)
        v_all_t[h, hd:, :] = jnp.where(pad_row == 0, 1.0, 0.0).astype(BF16)

    def scores(h, i, s_ref):
        q = q_ref[0, i * tq:(i + 1) * tq, h * hd:(h + 1) * hd]
        zero = jnp.zeros_like(q)
        qq = jnp.concatenate([jnp.where(lane < HEAD_DIM, q, zero), jnp.where(lane >= HEAD_DIM, q, zero)], axis=0)
        s_ref[...] = lax.dot_general(k_all[h], qq, contract_last, preferred_element_type=F32)

    def finish(h, i, s_ref):
        s = s_ref[...]
        p = jnp.exp2(s - jnp.max(s, axis=0, keepdims=True)).astype(BF16)
        acc = jnp.dot(v_all_t[h], p, preferred_element_type=F32)
        n0, n1 = acc[hd:hd + 1, :tq], acc[hd:hd + 1, tq:]
        o_t = acc[:hd, :tq] / n0 - acc[:hd, tq:] * (lam / n1)
        o_t = o_t * lax.rsqrt(jnp.mean(o_t * o_t, axis=0, keepdims=True) + SUBLN_EPS)
        o_ref[0, i * tq:(i + 1) * tq, h * hd:(h + 1) * hd] = (o_t * out_gain).T.astype(BF16)

    work = [(h, i) for i in range(n_tiles) for h in range(heads)]
    scores(*work[0], s_bufs[0])
    for n, (h, i) in enumerate(work):
        if n + 1 < len(work):
            scores(*work[n + 1], s_bufs[(n + 1) % len(s_bufs)])
        finish(h, i, s_bufs[n % len(s_bufs)])


def _diff_attention(q, k, v, kc, vc, lq1, lk1, lq2, lk2, subln_g, lam_init, tq, heads):
    b, l, dm = q.shape
    c = kc.shape[1]
    hd = 2 * HEAD_DIM
    vec = lambda n: pl.BlockSpec((1, n), lambda bi, h: (0, 0))
    per_group = lambda rows: pl.BlockSpec((1, rows, heads * hd), lambda bi, h: (bi, 0, h))
    score_buf = pltpu.VMEM((l + c, 2 * tq), F32)
    est = 2 * heads * (3 * l + 2 * c) * hd * 2 * 2 + 6 * 2 * tq * (l + c) * 4
    return pl.pallas_call(
        functools.partial(_diff_attn_kernel, lam_init=lam_init, tq=tq, heads=heads),
        grid=(b, dm // (heads * hd)),
        in_specs=[
            vec(HEAD_DIM), vec(HEAD_DIM), vec(HEAD_DIM), vec(HEAD_DIM),
            pl.BlockSpec((hd, 1), lambda bi, h: (0, 0)),
            per_group(l), per_group(l), per_group(l), per_group(c), per_group(c),
        ],
        out_specs=per_group(l),
        out_shape=jax.ShapeDtypeStruct((b, l, dm), BF16),
        scratch_shapes=[pltpu.VMEM((heads, l + c, hd), BF16),
                        pltpu.VMEM((heads, hd + BF16_SUBLANES, l + c), BF16), score_buf, score_buf],
        compiler_params=_params(est, 2),
        name="diff_attn",
    )(lq1, lk1, lq2, lk2, subln_g, q, k, v, kc, vc)


def _post_kernel(a_ref, x_ref, mod_ref, wo_ref, w1_ref, w2_ref, ln_ref, o_ref, *, ff_chunk):
    d = x_ref.shape[-1]
    gate1 = mod_ref[0, :, 2 * d:3 * d]
    shift2 = mod_ref[0, :, 3 * d:4 * d]
    scale2 = mod_ref[0, :, 4 * d:5 * d]
    gate2 = mod_ref[0, :, 5 * d:6 * d]
    ax = jnp.dot(a_ref[0], wo_ref[...], preferred_element_type=F32)
    x1 = _layer_norm(DEEPNORM_ALPHA * x_ref[0] + gate1 * ax, ln_ref[0:1, :], ln_ref[1:2, :])
    h = (x1 * (1.0 + scale2) + shift2).astype(BF16)
    fx = jnp.zeros_like(x1)
    for j in range(w1_ref.shape[1] // ff_chunk):
        u = jnp.dot(h, w1_ref[:, j * ff_chunk:(j + 1) * ff_chunk], preferred_element_type=F32)
        u = jnp.maximum(u, 0.0)
        fx = fx + jnp.dot((u * u).astype(BF16), w2_ref[j * ff_chunk:(j + 1) * ff_chunk, :],
                          preferred_element_type=F32)
    o_ref[0] = _layer_norm(DEEPNORM_ALPHA * x1 + gate2 * fx, ln_ref[2:3, :], ln_ref[3:4, :])


def _post(attn, x, mod, wo, w1, w2, ln, tm):
    nb, rows, d = x.shape
    dff = w1.shape[1]
    ff_chunk = 1024
    mod_map = (lambda b, i: (b, 0, 0)) if mod.shape[0] == nb and nb > 1 else (lambda b, i: (0, 0, 0))
    const = lambda shape: pl.BlockSpec(shape, lambda b, i: (0, 0), pipeline_mode=pl.Buffered(1))
    est = ((d * d + 2 * d * dff) * 2 + 2 * tm * d * (2 + 4 + 4)
           + tm * d * 4 * 4 + tm * ff_chunk * 6)
    return pl.pallas_call(
        functools.partial(_post_kernel, ff_chunk=ff_chunk),
        grid=(nb, rows // tm),
        in_specs=[
            pl.BlockSpec((1, tm, d), lambda b, i: (b, i, 0)),
            pl.BlockSpec((1, tm, d), lambda b, i: (b, i, 0)),
            pl.BlockSpec((1, 1, mod.shape[-1]), mod_map),
            const(wo.shape), const(w1.shape), const(w2.shape),
            pl.BlockSpec(ln.shape, lambda b, i: (0, 0)),
        ],
        out_specs=pl.BlockSpec((1, tm, d), lambda b, i: (b, i, 0)),
        out_shape=jax.ShapeDtypeStruct((nb, rows, d), F32),
        compiler_params=_params(est, 2),
        name="oproj_ln_mlp_ln",
    )(attn, x, mod, wo, w1, w2, ln)


def _rope_tables(seq):
    rows = seq // GRID_W
    row = jnp.repeat(jnp.arange(rows, dtype=F32), GRID_W)
    col = jnp.tile(jnp.arange(GRID_W, dtype=F32), rows)
    n_freq = HEAD_DIM // 4
    inv = ROPE_BASE ** (-jnp.arange(n_freq, dtype=F32) / n_freq)
    ang = jnp.concatenate([row[:, None] * inv, col[:, None] * inv], axis=-1)
    cos, sin = jnp.cos(ang), jnp.sin(ang)
    reps = V7X_LANES // HEAD_DIM
    cos_t = jnp.tile(jnp.concatenate([cos, cos], axis=-1), (1, reps))
    sin_t = jnp.tile(jnp.concatenate([-sin, sin], axis=-1), (1, reps))
    return cos_t, sin_t


def kernel(x, c, ctx, c_ctx, w_ada, b_ada, ln1_g, ln1_b, ln2_g, ln2_b, a_wq, a_wk, a_wv, a_wo, a_sink,
           b_wq, b_wk, b_wv, b_wo, b_lq1, b_lk1, b_lq2, b_lk2, b_subln_g, mlp_w1, mlp_w2):
    batch, seq, d = x.shape
    ctx_len = ctx.shape[1]
    tables = _rope_tables(seq)
    bf = lambda w: w.astype(BF16)

    cond = jnp.concatenate([c, c_ctx[None, :], jnp.zeros((MOD_ROWS - batch - 1, d), F32)], axis=0)
    mod = _ada(cond, w_ada, b_ada[:, None, :])

    ctx_rows = 512

    def ctx_project(ctx, mod_c, weights, scales):
        outs = _project(ctx.reshape(1, batch * ctx_len, d), mod_c, weights, None, scales, tm=ctx_rows)
        return [o.reshape(batch, ctx_len, o.shape[-1]) for o in outs]

    for i in range(DEPTH):
        need_ctx = i < DEPTH - 1
        mod_x = mod[i, :batch][:, None, :]
        mod_c = mod[i, batch:batch + 1][:, None, :]
        ln = jnp.stack([ln1_g[i], ln1_b[i], ln2_g[i], ln2_b[i]], axis=0)
        j = i // N_MIXERS
        if i % N_MIXERS == 0:
            wq = a_wq[j].reshape(d, A_KV_HEADS, A_GROUP, HEAD_DIM).swapaxes(1, 2).reshape(d, d)
            wo = a_wo[j].reshape(A_KV_HEADS, A_GROUP, HEAD_DIM, d).swapaxes(0, 1).reshape(d, d)
            wq, wk, wv, wo = bf(wq), bf(a_wk[j]), bf(a_wv[j]), bf(wo)
            q, k, v = _project(x, mod_x, [wq, wk, wv], tables, [QK_SCALE * LOG2_E, 1.0, None], tm=1024)
            if need_ctx:
                qc, kc, vc = ctx_project(ctx, mod_c, [wq, wk, wv], [QK_SCALE * LOG2_E, None, None])
            else:
                kc, vc = ctx_project(ctx, mod_c, [wk, wv], [None, None])
            attn_x = _window_attention(q, k, v, kc, vc, a_sink[j], q_blocks=4)
            if need_ctx:
                attn_c = _ctx_attention(qc, kc, vc, a_sink[j])
        else:
            lam_init = 0.8 - 0.6 * math.exp(-0.3 * i)
            wq, wk, wv, wo = bf(b_wq[j]), bf(b_wk[j]), bf(b_wv[j]), bf(b_wo[j])
            q, k, v = _project(x, mod_x, [wq, wk, wv], tables, [QK_SCALE * LOG2_E, 1.0, None], tm=1024)
            if need_ctx:
                raise NotImplementedError("context queries through differential attention")
            kc, vc = ctx_project(ctx, mod_c, [wk, wv], [None, None])
            attn_x = _diff_attention(q, k, v, kc, vc, b_lq1[j][None], b_lk1[j][None], b_lq2[j][None],
                                     b_lk2[j][None], b_subln_g[j][:, None], lam_init, tq=512, heads=2)
        w1, w2 = bf(mlp_w1[i]), bf(mlp_w2[i])
        x = _post(attn_x, x, mod_x, wo, w1, w2, ln, tm=512)
        if need_ctx:
            flat = (1, batch * ctx_len, d)
            ctx = _post(attn_c.reshape(flat), ctx.reshape(flat), mod_c, wo, w1, w2, ln, tm=ctx_rows)
            ctx = ctx.reshape(batch, ctx_len, d)
    return x
```

```python
import functools
import math

import jax
import jax.numpy as jnp
from jax import lax
from jax.experimental import pallas as pl
from jax.experimental.pallas import tpu as pltpu

D_MODEL = 1024
DEPTH = 2
GRID_W = 64
N_MIXERS = 2
BLOCK = 128
HEAD_DIM = 64
A_HEADS = D_MODEL // HEAD_DIM
A_KV_HEADS = max(1, A_HEADS // 8)
A_GROUP = A_HEADS // A_KV_HEADS
B_HEADS = D_MODEL // (2 * HEAD_DIM)
D_FF = 4 * D_MODEL
N_MOD = 6
ROPE_BASE = 10000.0
LN_EPS = 1e-5
SUBLN_EPS = 1e-5
NEG_INF = -1e30
DEEPNORM_ALPHA = (2 * DEPTH) ** 0.25
QK_SCALE = HEAD_DIM ** -0.5
LOG2_E = math.log2(math.e)

V7X_LANES = 128
BF16_SUBLANES = 16
V7X_VMEM_BYTES = 64 * 1024 * 1024
MOD_ROWS = 24

BF16 = jnp.bfloat16
F32 = jnp.float32


def _vmem_limit(estimate_bytes):
    return int(min(max(2 * estimate_bytes, 16 * 1024 * 1024), V7X_VMEM_BYTES - 8 * 1024 * 1024))


def _params(estimate_bytes, n_grid, fuse_inputs=None):
    return pltpu.CompilerParams(
        dimension_semantics=("arbitrary",) * n_grid,
        vmem_limit_bytes=_vmem_limit(estimate_bytes),
        allow_input_fusion=fuse_inputs,
    )


def _layer_norm(y, g, b):
    mu = jnp.mean(y, axis=-1, keepdims=True)
    yc = y - mu
    var = jnp.mean(yc * yc, axis=-1, keepdims=True)
    return yc * lax.rsqrt(var + LN_EPS) * g + b


def _ada_kernel(c_ref, w_ref, b_ref, o_ref):
    c = c_ref[...]
    h = (c * jax.nn.sigmoid(c)).astype(BF16)
    acc = jnp.dot(h, w_ref[0].astype(BF16), preferred_element_type=F32)
    o_ref[0] = acc + b_ref[0]


def _ada(cond, w_ada, b_ada):
    depth, d, n = w_ada.shape
    tn = 1024
    est = 2 * d * tn * 4 + d * tn * 2 + MOD_ROWS * (d + 2 * tn) * 4
    return pl.pallas_call(
        _ada_kernel,
        grid=(depth, n // tn),
        in_specs=[
            pl.BlockSpec((MOD_ROWS, d), lambda i, j: (0, 0)),
            pl.BlockSpec((1, d, tn), lambda i, j: (i, 0, j)),
            pl.BlockSpec((1, 1, tn), lambda i, j: (i, 0, j)),
        ],
        out_specs=pl.BlockSpec((1, MOD_ROWS, tn), lambda i, j: (i, 0, j)),
        out_shape=jax.ShapeDtypeStruct((depth, MOD_ROWS, n), F32),
        compiler_params=_params(est, 2),
        name="ada_mod",
    )(cond, w_ada, b_ada)


def _rope(y, cos, sin_signed, first_half):
    half = HEAD_DIM // 2
    outs = []
    for g in range(y.shape[1] // V7X_LANES):
        yg = y[:, g * V7X_LANES:(g + 1) * V7X_LANES]
        partner = jnp.where(first_half,
                            pltpu.roll(yg, V7X_LANES - half, 1),
                            pltpu.roll(yg, half, 1))
        outs.append(yg * cos + partner * sin_signed)
    return outs


def _proj_kernel(*refs, n_out, rope, scales):
    x_ref, mod_ref = refs[0], refs[1]
    w_refs = refs[2:2 + n_out]
    pos = 2 + n_out
    if rope:
        cos_ref, sin_ref = refs[pos], refs[pos + 1]
        pos += 2
    o_refs = refs[pos:pos + n_out]
    d = x_ref.shape[-1]
    shift = mod_ref[0, :, 0:d]
    scale = mod_ref[0, :, d:2 * d]
    h = (x_ref[0] * (1.0 + scale) + shift).astype(BF16)
    if rope:
        cos = cos_ref[...]
        sin_signed = sin_ref[...]
        lane = lax.broadcasted_iota(jnp.int32, cos.shape, 1)
        first_half = (lane % HEAD_DIM) < (HEAD_DIM // 2)
    for j in range(n_out):
        y = jnp.dot(h, w_refs[j][...], preferred_element_type=F32)
        roped = rope and scales[j] is not None
        if roped:
            parts = _rope(y, cos, sin_signed, first_half)
            for g, part in enumerate(parts):
                if scales[j] != 1.0:
                    part = part * scales[j]
                o_refs[j][0, :, g * V7X_LANES:(g + 1) * V7X_LANES] = part.astype(BF16)
        else:
            if scales[j] is not None and scales[j] != 1.0:
                y = y * scales[j]
            o_refs[j][0] = y.astype(BF16)


def _project(x, mod, weights, rope_tables, scales, tm):
    nb, rows, d = x.shape
    n_out = len(weights)
    rope = rope_tables is not None
    mod_map = (lambda b, i: (b, 0, 0)) if mod.shape[0] == nb and nb > 1 else (lambda b, i: (0, 0, 0))
    in_specs = [
        pl.BlockSpec((1, tm, d), lambda b, i: (b, i, 0)),
        pl.BlockSpec((1, 1, mod.shape[-1]), mod_map),
    ]
    for w in weights:
        in_specs.append(pl.BlockSpec(w.shape, lambda b, i: (0, 0)))
    args = [x, mod] + list(weights)
    if rope:
        in_specs += [pl.BlockSpec((tm, V7X_LANES), lambda b, i: (i, 0))] * 2
        args += list(rope_tables)
    n_tot = sum(w.shape[1] for w in weights)
    est = 2 * tm * d * 4 + 2 * d * n_tot * 2 + 2 * tm * n_tot * 2 + tm * d * 2 + 2 * tm * n_tot * 4
    return pl.pallas_call(
        functools.partial(_proj_kernel, n_out=n_out, rope=rope, scales=tuple(scales)),
        grid=(nb, rows // tm),
        in_specs=in_specs,
        out_specs=[pl.BlockSpec((1, tm, w.shape[1]), lambda b, i: (b, i, 0)) for w in weights],
        out_shape=[jax.ShapeDtypeStruct((nb, rows, w.shape[1]), BF16) for w in weights],
        compiler_params=_params(est, 2),
        name="mod_proj_rope" if rope else "mod_proj",
    )(*args)


def _gqa_block(q, keys, vals, masks, sink_ref, o_ref, row0):
    tq = q.shape[0]
    width = A_KV_HEADS * HEAD_DIM
    contract_last = (((1,), (1,)), ((), ()))
    qs = jnp.concatenate([q[:, h * width:(h + 1) * width] for h in range(A_GROUP)], axis=0)
    q_lane = lax.broadcasted_iota(jnp.int32, qs.shape, 1)
    vals_t = vals.astype(F32).T.astype(BF16)
    masks = [(r0, jnp.concatenate([valid] * A_GROUP, axis=1)) for r0, valid in masks]
    outs = []
    for g in range(A_KV_HEADS):
        qg = jnp.where((q_lane >= g * HEAD_DIM) & (q_lane < (g + 1) * HEAD_DIM), qs, jnp.zeros_like(qs))
        s = lax.dot_general(keys, qg, contract_last, preferred_element_type=F32)
        pieces, row = [], 0
        for r0, valid in masks:
            if r0 > row:
                pieces.append(s[row:r0])
            pieces.append(jnp.where(valid, s[r0:r0 + valid.shape[0]], NEG_INF))
            row = r0 + valid.shape[0]
        if masks:
            s = jnp.concatenate(pieces + [s[row:]], axis=0)
        sink = jnp.concatenate(
            [jnp.full((1, tq), sink_ref[g * A_GROUP + h] * LOG2_E, F32) for h in range(A_GROUP)], axis=1)
        m = jnp.maximum(jnp.max(s, axis=0, keepdims=True), sink)
        p = jnp.exp2(s - m)
        inv = 1.0 / (jnp.sum(p, axis=0, keepdims=True) + jnp.exp2(sink - m))
        acc = jnp.dot(vals_t[g * HEAD_DIM:(g + 1) * HEAD_DIM, :], p.astype(BF16), preferred_element_type=F32)
        outs.append(acc * inv)
    out_t = jnp.concatenate(outs, axis=0)
    for h in range(A_GROUP):
        o_ref[0, row0:row0 + tq, h * width:(h + 1) * width] = out_t[:, h * tq:(h + 1) * tq].T.astype(BF16)


def _window_attn_kernel(sink_ref, q_ref, *refs, q_blocks):
    n_kv = q_blocks + 2
    k_refs, v_refs = refs[:n_kv], refs[n_kv:2 * n_kv]
    kc_ref, vc_ref, o_ref = refs[2 * n_kv:]
    first = pl.program_id(1) * q_blocks
    last = pl.num_programs(1) * q_blocks - 1
    key = lax.broadcasted_iota(jnp.int32, (BLOCK, BLOCK), 0)
    qry = lax.broadcasted_iota(jnp.int32, (BLOCK, BLOCK), 1)
    for t in range(q_blocks):
        masks = [
            (0, (key >= qry) & (first + t > 0)),
            (2 * BLOCK, (key <= qry) & (first + t < last)),
        ]
        keys = jnp.concatenate([k_refs[t][0], k_refs[t + 1][0], k_refs[t + 2][0], kc_ref[0]], axis=0)
        vals = jnp.concatenate([v_refs[t][0], v_refs[t + 1][0], v_refs[t + 2][0], vc_ref[0]], axis=0)
        _gqa_block(q_ref[0, t * BLOCK:(t + 1) * BLOCK, :], keys, vals, masks, sink_ref, o_ref, t * BLOCK)


def _window_attention(q, k, v, kc, vc, sink, q_blocks):
    b, l, dq = q.shape
    dk = k.shape[-1]
    c = kc.shape[1]
    nblk = l // BLOCK
    tq = q_blocks * BLOCK

    def kv_spec(j):
        return pl.BlockSpec((1, BLOCK, dk), lambda bi, n: (bi, jnp.clip(n * q_blocks - 1 + j, 0, nblk - 1), 0))

    kv_specs = [kv_spec(j) for j in range(q_blocks + 2)]
    ctx_spec = pl.BlockSpec((1, c, dk), lambda bi, n: (bi, 0, 0))
    est = 4 * tq * dq * 2 + 8 * (tq + 2 * BLOCK + c) * dk * 2 + 6 * q_blocks * A_GROUP * BLOCK * (3 * BLOCK + c) * 4
    return pl.pallas_call(
        functools.partial(_window_attn_kernel, q_blocks=q_blocks),
        grid=(b, nblk // q_blocks),
        in_specs=[pl.BlockSpec(memory_space=pltpu.SMEM), pl.BlockSpec((1, tq, dq), lambda bi, n: (bi, n, 0))]
        + kv_specs + kv_specs + [ctx_spec, ctx_spec],
        out_specs=pl.BlockSpec((1, tq, dq), lambda bi, n: (bi, n, 0)),
        out_shape=jax.ShapeDtypeStruct((b, l, dq), BF16),
        compiler_params=_params(est, 2),
        name="window_gqa_attn",
    )(sink, q, *([k] * (q_blocks + 2)), *([v] * (q_blocks + 2)), kc, vc)


def _ctx_attn_kernel(sink_ref, q_ref, kc_ref, vc_ref, o_ref):
    _gqa_block(q_ref[0], kc_ref[0], vc_ref[0], [], sink_ref, o_ref, 0)


def _ctx_attention(qc, kc, vc, sink):
    b, c, dq = qc.shape
    dk = kc.shape[-1]
    est = 4 * c * dq * 2 + 4 * c * dk * 2 + 6 * A_GROUP * c * c * 4
    return pl.pallas_call(
        _ctx_attn_kernel,
        grid=(b,),
        in_specs=[
            pl.BlockSpec(memory_space=pltpu.SMEM),
            pl.BlockSpec((1, c, dq), lambda bi: (bi, 0, 0)),
            pl.BlockSpec((1, c, dk), lambda bi: (bi, 0, 0)),
            pl.BlockSpec((1, c, dk), lambda bi: (bi, 0, 0)),
        ],
        out_specs=pl.BlockSpec((1, c, dq), lambda bi: (bi, 0, 0)),
        out_shape=jax.ShapeDtypeStruct((b, c, dq), BF16),
        compiler_params=_params(est, 1),
        name="ctx_gqa_attn",
    )(sink, qc, kc, vc)


def _diff_attn_kernel(lq1_ref, lk1_ref, lq2_ref, lk2_ref, g_ref, q_ref, k_ref, v_ref, kc_ref, vc_ref,
                      o_ref, k_all, v_all_t, *s_bufs, lam_init, tq, heads):
    seq = k_ref.shape[1]
    n_tiles = seq // tq
    hd = 2 * HEAD_DIM
    keys = k_all.shape[1]
    lam = (jnp.exp(jnp.sum(lq1_ref[...] * lk1_ref[...], axis=-1, keepdims=True))
           - jnp.exp(jnp.sum(lq2_ref[...] * lk2_ref[...], axis=-1, keepdims=True)) + lam_init)
    lane = lax.broadcasted_iota(jnp.int32, (tq, hd), 1)
    contract_last = (((1,), (1,)), ((), ()))
    out_gain = g_ref[...] * (1.0 - lam_init)

    pad_row = lax.broadcasted_iota(jnp.int32, (v_all_t.shape[1] - hd, keys), 0)
    for h in range(heads):
        cols = slice(h * hd, (h + 1) * hd)
        k_all[h, :seq, :] = k_ref[0, :, cols]
        k_all[h, seq:, :] = kc_ref[0, :, cols]
        v_all_t[h, :hd, :seq] = v_ref[0, :, cols].astype(F32).T.astype(BF16)
        v_all_t[h, :hd, seq:] = vc_ref[0, :, cols].astype(F32).T.astype(BF16)
        v_all_t[h, hd:, :] = jnp.where(pad_row == 0, 1.0, 0.0).astype(BF16)

    def scores(h, i, s_ref):
        q = q_ref[0, i * tq:(i + 1) * tq, h * hd:(h + 1) * hd]
        zero = jnp.zeros_like(q)
        qq = jnp.concatenate([jnp.where(lane < HEAD_DIM, q, zero), jnp.where(lane >= HEAD_DIM, q, zero)], axis=0)
        s_ref[...] = lax.dot_general(k_all[h], qq, contract_last, preferred_element_type=F32)

    def finish(h, i, s_ref):
        s = s_ref[...]
        p = jnp.exp2(s - jnp.max(s, axis=0, keepdims=True)).astype(BF16)
        acc = jnp.dot(v_all_t[h], p, preferred_element_type=F32)
        n0, n1 = acc[hd:hd + 1, :tq], acc[hd:hd + 1, tq:]
        o_t = acc[:hd, :tq] / n0 - acc[:hd, tq:] * (lam / n1)
        o_t = o_t * lax.rsqrt(jnp.mean(o_t * o_t, axis=0, keepdims=True) + SUBLN_EPS)
        o_ref[0, i * tq:(i + 1) * tq, h * hd:(h + 1) * hd] = (o_t * out_gain).T.astype(BF16)

    work = [(h, i) for i in range(n_tiles) for h in range(heads)]
    scores(*work[0], s_bufs[0])
    for n, (h, i) in enumerate(work):
        if n + 1 < len(work):
            scores(*work[n + 1], s_bufs[(n + 1) % len(s_bufs)])
        finish(h, i, s_bufs[n % len(s_bufs)])


def _diff_attention(q, k, v, kc, vc, lq1, lk1, lq2, lk2, subln_g, lam_init, tq, heads):
    b, l, dm = q.shape
    c = kc.shape[1]
    hd = 2 * HEAD_DIM
    vec = lambda n: pl.BlockSpec((1, n), lambda bi, h: (0, 0))
    per_group = lambda rows: pl.BlockSpec((1, rows, heads * hd), lambda bi, h: (bi, 0, h))
    score_buf = pltpu.VMEM((l + c, 2 * tq), F32)
    est = 2 * heads * (3 * l + 2 * c) * hd * 2 * 2 + 6 * 2 * tq * (l + c) * 4
    return pl.pallas_call(
        functools.partial(_diff_attn_kernel, lam_init=lam_init, tq=tq, heads=heads),
        grid=(b, dm // (heads * hd)),
        in_specs=[
            vec(HEAD_DIM), vec(HEAD_DIM), vec(HEAD_DIM), vec(HEAD_DIM),
            pl.BlockSpec((hd, 1), lambda bi, h: (0, 0)),
            per_group(l), per_group(l), per_group(l), per_group(c), per_group(c),
        ],
        out_specs=per_group(l),
        out_shape=jax.ShapeDtypeStruct((b, l, dm), BF16),
        scratch_shapes=[pltpu.VMEM((heads, l + c, hd), BF16),
                        pltpu.VMEM((heads, hd + BF16_SUBLANES, l + c), BF16), score_buf, score_buf],
        compiler_params=_params(est, 2),
        name="diff_attn",
    )(lq1, lk1, lq2, lk2, subln_g, q, k, v, kc, vc)


def _post_kernel(a_ref, x_ref, mod_ref, wo_ref, w1_ref, w2_ref, ln_ref, o_ref, *, ff_chunk):
    d = x_ref.shape[-1]
    gate1 = mod_ref[0, :, 2 * d:3 * d]
    shift2 = mod_ref[0, :, 3 * d:4 * d]
    scale2 = mod_ref[0, :, 4 * d:5 * d]
    gate2 = mod_ref[0, :, 5 * d:6 * d]
    ax = jnp.dot(a_ref[0], wo_ref[...], preferred_element_type=F32)
    x1 = _layer_norm(DEEPNORM_ALPHA * x_ref[0] + gate1 * ax, ln_ref[0:1, :], ln_ref[1:2, :])
    h = (x1 * (1.0 + scale2) + shift2).astype(BF16)
    fx = jnp.zeros_like(x1)
    for j in range(w1_ref.shape[1] // ff_chunk):
        u = jnp.dot(h, w1_ref[:, j * ff_chunk:(j + 1) * ff_chunk], preferred_element_type=F32)
        u = jnp.maximum(u, 0.0)
        fx = fx + jnp.dot((u * u).astype(BF16), w2_ref[j * ff_chunk:(j + 1) * ff_chunk, :],
                          preferred_element_type=F32)
    o_ref[0] = _layer_norm(DEEPNORM_ALPHA * x1 + gate2 * fx, ln_ref[2:3, :], ln_ref[3:4, :])


def _post(attn, x, mod, wo, w1, w2, ln, tm):
    nb, rows, d = x.shape
    dff = w1.shape[1]
    ff_chunk = 1024
    mod_map = (lambda b, i: (b, 0, 0)) if mod.shape[0] == nb and nb > 1 else (lambda b, i: (0, 0, 0))
    const = lambda shape: pl.BlockSpec(shape, lambda b, i: (0, 0), pipeline_mode=pl.Buffered(1))
    est = ((d * d + 2 * d * dff) * 2 + 2 * tm * d * (2 + 4 + 4)
           + tm * d * 4 * 4 + tm * ff_chunk * 6)
    return pl.pallas_call(
        functools.partial(_post_kernel, ff_chunk=ff_chunk),
        grid=(nb, rows // tm),
        in_specs=[
            pl.BlockSpec((1, tm, d), lambda b, i: (b, i, 0)),
            pl.BlockSpec((1, tm, d), lambda b, i: (b, i, 0)),
            pl.BlockSpec((1, 1, mod.shape[-1]), mod_map),
            const(wo.shape), const(w1.shape), const(w2.shape),
            pl.BlockSpec(ln.shape, lambda b, i: (0, 0)),
        ],
        out_specs=pl.BlockSpec((1, tm, d), lambda b, i: (b, i, 0)),
        out_shape=jax.ShapeDtypeStruct((nb, rows, d), F32),
        compiler_params=_params(est, 2, fuse_inputs=[False, False, False, True, True, True, False]),
        name="oproj_ln_mlp_ln",
    )(attn, x, mod, wo, w1, w2, ln)


def _rope_tables(seq):
    rows = seq // GRID_W
    row = jnp.repeat(jnp.arange(rows, dtype=F32), GRID_W)
    col = jnp.tile(jnp.arange(GRID_W, dtype=F32), rows)
    n_freq = HEAD_DIM // 4
    inv = ROPE_BASE ** (-jnp.arange(n_freq, dtype=F32) / n_freq)
    ang = jnp.concatenate([row[:, None] * inv, col[:, None] * inv], axis=-1)
    cos, sin = jnp.cos(ang), jnp.sin(ang)
    reps = V7X_LANES // HEAD_DIM
    cos_t = jnp.tile(jnp.concatenate([cos, cos], axis=-1), (1, reps))
    sin_t = jnp.tile(jnp.concatenate([-sin, sin], axis=-1), (1, reps))
    return cos_t, sin_t


def kernel(x, c, ctx, c_ctx, w_ada, b_ada, ln1_g, ln1_b, ln2_g, ln2_b, a_wq, a_wk, a_wv, a_wo, a_sink,
           b_wq, b_wk, b_wv, b_wo, b_lq1, b_lk1, b_lq2, b_lk2, b_subln_g, mlp_w1, mlp_w2):
    batch, seq, d = x.shape
    ctx_len = ctx.shape[1]
    tables = _rope_tables(seq)
    bf = lambda w: w.astype(BF16)

    cond = jnp.concatenate([c, c_ctx[None, :], jnp.zeros((MOD_ROWS - batch - 1, d), F32)], axis=0)
    mod = _ada(cond, w_ada, b_ada[:, None, :])

    ctx_rows = 512

    def ctx_project(ctx, mod_c, weights, scales):
        outs = _project(ctx.reshape(1, batch * ctx_len, d), mod_c, weights, None, scales, tm=ctx_rows)
        return [o.reshape(batch, ctx_len, o.shape[-1]) for o in outs]

    for i in range(DEPTH):
        need_ctx = i < DEPTH - 1
        mod_x = mod[i, :batch][:, None, :]
        mod_c = mod[i, batch:batch + 1][:, None, :]
        ln = jnp.stack([ln1_g[i], ln1_b[i], ln2_g[i], ln2_b[i]], axis=0)
        j = i // N_MIXERS
        if i % N_MIXERS == 0:
            wq = a_wq[j].reshape(d, A_KV_HEADS, A_GROUP, HEAD_DIM).swapaxes(1, 2).reshape(d, d)
            wo = a_wo[j].reshape(A_KV_HEADS, A_GROUP, HEAD_DIM, d).swapaxes(0, 1).reshape(d, d)
            wq, wk, wv, wo = bf(wq), bf(a_wk[j]), bf(a_wv[j]), bf(wo)
            q, k, v = _project(x, mod_x, [wq, wk, wv], tables, [QK_SCALE * LOG2_E, 1.0, None], tm=1024)
            if need_ctx:
                qc, kc, vc = ctx_project(ctx, mod_c, [wq, wk, wv], [QK_SCALE * LOG2_E, None, None])
            else:
                kc, vc = ctx_project(ctx, mod_c, [wk, wv], [None, None])
            attn_x = _window_attention(q, k, v, kc, vc, a_sink[j], q_blocks=4)
            if need_ctx:
                attn_c = _ctx_attention(qc, kc, vc, a_sink[j])
        else:
            lam_init = 0.8 - 0.6 * math.exp(-0.3 * i)
            wq, wk, wv, wo = bf(b_wq[j]), bf(b_wk[j]), bf(b_wv[j]), bf(b_wo[j])
            q, k, v = _project(x, mod_x, [wq, wk, wv], tables, [QK_SCALE * LOG2_E, 1.0, None], tm=1024)
            if need_ctx:
                raise NotImplementedError("context queries through differential attention")
            kc, vc = ctx_project(ctx, mod_c, [wk, wv], [None, None])
            attn_x = _diff_attention(q, k, v, kc, vc, b_lq1[j][None], b_lk1[j][None], b_lq2[j][None],
                                     b_lk2[j][None], b_subln_g[j][:, None], lam_init, tq=512, heads=2)
        w1, w2 = bf(mlp_w1[i]), bf(mlp_w2[i])
        x = _post(attn_x, x, mod_x, wo, w1, w2, ln, tm=512)
        if need_ctx:
            flat = (1, batch * ctx_len, d)
            ctx = _post(attn_c.reshape(flat), ctx.reshape(flat), mod_c, wo, w1, w2, ln, tm=ctx_rows)
            ctx = ctx.reshape(batch, ctx_len, d)
    return x
```
